```python
import jax, jax.numpy as jnp
from jax import lax
import numpy as np

D_MODEL = 2048
BATCH = 4
SEQ = 2048
DEPTH = 1
DEC_BATCH = 128
DEC_SEQ = 8
PAST_LEN = 16384
PAGE_SIZE = 128

EXPAND = 2
D_MIX = EXPAND * D_MODEL
W_CONV = D_MIX // 2
W_HGRN = D_MIX - W_CONV
CONV_GROUPS = 16
CONV_WIDTH = 3
HGRN_DK = 128
HGRN_HEADS = W_HGRN // HGRN_DK
HGRN_DV = W_HGRN // HGRN_HEADS
CHUNK = 64
N_PROJ = 4 * W_CONV + 4 * W_HGRN
DEEPNORM_ALPHA = (2.0 * DEPTH) ** 0.25
DEEPNORM_BETA = (8.0 * DEPTH) ** -0.25
EPS = 1e-5

kernel_name = "hymba_conv_hgrn2_deepnorm_step"

_SPLITS = [W_CONV, 2 * W_CONV, 3 * W_CONV, 4 * W_CONV,
           4 * W_CONV + W_HGRN, 4 * W_CONV + 2 * W_HGRN, 4 * W_CONV + 3 * W_HGRN]


def _group_rms(x, gain, n_groups):
    shp = x.shape
    xf = x.astype(jnp.float32).reshape(shp[:-1] + (n_groups, shp[-1] // n_groups))
    xf = xf * lax.rsqrt(jnp.mean(xf * xf, axis=-1, keepdims=True) + EPS)
    return xf.reshape(shp) * gain.astype(jnp.float32)


def _layer_norm(x, gain, bias):
    xf = x.astype(jnp.float32)
    mu = jnp.mean(xf, axis=-1, keepdims=True)
    xc = xf - mu
    var = jnp.mean(xc * xc, axis=-1, keepdims=True)
    return xc * lax.rsqrt(var + EPS) * gain.astype(jnp.float32) + bias.astype(jnp.float32)


def _short_conv(u, buf, w):
    T = u.shape[1]
    ext = jnp.concatenate([buf.astype(u.dtype), u], axis=1)
    y = sum(ext[:, j:j + T] * w[j] for j in range(CONV_WIDTH))
    new_buf = ext[:, ext.shape[1] - (CONV_WIDTH - 1):]
    return y, new_buf


def _hgrn2_chunked(q, k, v, g, s0):
    bsz, T = q.shape[0], q.shape[1]
    n_c = -(-T // CHUNK)
    pad = n_c * CHUNK - T
    if pad:
        pw = ((0, 0), (0, pad), (0, 0), (0, 0))
        q, k, v, g = (jnp.pad(a, pw) for a in (q, k, v, g))
    rs = lambda a: a.reshape(bsz, n_c, CHUNK, HGRN_HEADS, a.shape[-1])
    q, k, v, g = rs(q), rs(k), rs(v), rs(g)
    b = jnp.cumsum(g, axis=2)
    b_last = b[:, :, -1]
    q_e = q * jnp.exp(b)
    k_e = k * jnp.exp(-b)
    k_t = k * jnp.exp(b_last[:, :, None] - b)
    scores = jnp.einsum('bnthk,bnshk->bnhts', q_e, k_e)
    causal = jnp.tril(jnp.ones((CHUNK, CHUNK), dtype=bool))
    scores = jnp.where(causal, scores, 0.0)
    o_intra = jnp.einsum('bnhts,bnshv->bnthv', scores, v)

    def step(S, xs):
        qe_c, kt_c, v_c, dl_c = xs
        o = jnp.einsum('bthk,bhkv->bthv', qe_c, S)
        S = dl_c[..., None] * S + jnp.einsum('bshk,bshv->bhkv', kt_c, v_c)
        return S, o

    xs = (jnp.moveaxis(q_e, 1, 0), jnp.moveaxis(k_t, 1, 0),
          jnp.moveaxis(v, 1, 0), jnp.moveaxis(jnp.exp(b_last), 1, 0))
    s_fin, o_inter = lax.scan(step, s0.astype(jnp.float32), xs)
    o = o_intra + jnp.moveaxis(o_inter, 0, 1)
    o = o.reshape(bsz, n_c * CHUNK, HGRN_HEADS, HGRN_DV)[:, :T]
    return o, s_fin


def _mixer_layer(x, conv_buf, hgrn_s, w_in, conv_w, norm_a, lb, norm_b, w_out, ln_g, ln_b):
    bsz, T, _ = x.shape
    proj = jnp.einsum('btd,de->bte', x, w_in)
    v_a, b_a, c_a, z_a, q_b, f_b, i_b, z_b = jnp.split(proj, _SPLITS, axis=-1)

    conv_out, new_buf = _short_conv(c_a * v_a, conv_buf, conv_w)
    y_a = _group_rms(b_a * conv_out, norm_a, CONV_GROUPS) * jax.nn.silu(z_a.astype(jnp.float32))

    hs = lambda a: a.astype(jnp.float32).reshape(bsz, T, HGRN_HEADS, -1)
    q = jax.nn.silu(hs(q_b)) * (HGRN_DK ** -0.5)
    f = lb + (1.0 - lb) * jax.nn.sigmoid(hs(f_b))
    o, new_s = _hgrn2_chunked(q, 1.0 - f, hs(i_b), jnp.log(f), hgrn_s)
    y_b = _group_rms(o.reshape(bsz, T, W_HGRN), norm_b, HGRN_HEADS) * jax.nn.silu(z_b.astype(jnp.float32))

    mix = jnp.concatenate([y_a, y_b], axis=-1).astype(x.dtype)
    h = jnp.einsum('bte,ed->btd', mix, w_out)
    y = _layer_norm(DEEPNORM_ALPHA * x.astype(jnp.float32) + h.astype(jnp.float32), ln_g, ln_b)
    return y.astype(x.dtype), new_buf.astype(conv_buf.dtype), new_s.astype(hgrn_s.dtype)


def setup_inputs(seed: int = 0) -> dict:
    key = jax.random.key(seed)
    ks = jax.random.split(key, 12)
    x_prompt = jax.random.normal(ks[0], (BATCH, SEQ, D_MODEL), jnp.float32)
    x_sample = jax.random.normal(ks[1], (DEC_BATCH, DEC_SEQ, D_MODEL), jnp.float32)
    state_conv = 0.3 * jax.random.normal(ks[2], (DEPTH, DEC_BATCH, CONV_WIDTH - 1, W_CONV), jnp.float32)
    state_hgrn = 0.1 * jax.random.normal(ks[3], (DEPTH, DEC_BATCH, HGRN_HEADS, HGRN_DK, HGRN_DV), jnp.float32)
    col_scale = jnp.concatenate([
        jnp.full((W_CONV,), DEEPNORM_BETA, jnp.float32), jnp.ones((3 * W_CONV,), jnp.float32),
        jnp.ones((2 * W_HGRN,), jnp.float32), jnp.full((W_HGRN,), DEEPNORM_BETA, jnp.float32),
        jnp.ones((W_HGRN,), jnp.float32)])
    w_in = jax.random.normal(ks[4], (DEPTH, D_MODEL, N_PROJ), jnp.float32) * (D_MODEL ** -0.5) * col_scale
    conv_w = jax.random.normal(ks[5], (DEPTH, CONV_WIDTH, W_CONV), jnp.float32) * (CONV_WIDTH ** -0.5)
    norm_a = 1.0 + 0.01 * jax.random.normal(ks[6], (DEPTH, W_CONV), jnp.float32)
    lb_logits = 0.1 * jax.random.normal(ks[7], (DEPTH + 1, W_HGRN), jnp.float32)
    norm_b = 1.0 + 0.01 * jax.random.normal(ks[8], (DEPTH, W_HGRN), jnp.float32)
    w_out = jax.random.normal(ks[9], (DEPTH, D_MIX, D_MODEL), jnp.float32) * (D_MIX ** -0.5) * DEEPNORM_BETA
    ln_gain = 1.0 + 0.01 * jax.random.normal(ks[10], (DEPTH, D_MODEL), jnp.float32)
    ln_bias = 0.01 * jax.random.normal(ks[11], (DEPTH, D_MODEL), jnp.float32)
    return {"x_prompt": x_prompt, "x_sample": x_sample, "state_conv": state_conv,
            "state_hgrn": state_hgrn, "w_in": w_in, "conv_w": conv_w, "norm_a": norm_a,
            "lb_logits": lb_logits, "norm_b": norm_b, "w_out": w_out,
            "ln_gain": ln_gain, "ln_bias": ln_bias}


def reference(x_prompt, x_sample, state_conv, state_hgrn, w_in, conv_w, norm_a, lb_logits,
              norm_b, w_out, ln_gain, ln_bias):
    lb_all = jnp.cumsum(jax.nn.softmax(lb_logits.astype(jnp.float32), axis=0), axis=0)
    yp, ys = x_prompt, x_sample
    conv_p_list, hgrn_p_list, conv_s_list, hgrn_s_list = [], [], [], []
    for l in range(DEPTH):
        lb = lb_all[l].reshape(HGRN_HEADS, HGRN_DK)
        params = (w_in[l], conv_w[l], norm_a[l], lb, norm_b[l], w_out[l], ln_gain[l], ln_bias[l])
        zero_buf = jnp.zeros((BATCH, CONV_WIDTH - 1, W_CONV), state_conv.dtype)
        zero_s = jnp.zeros((BATCH, HGRN_HEADS, HGRN_DK, HGRN_DV), state_hgrn.dtype)
        yp, cb_p, s_p = _mixer_layer(yp, zero_buf, zero_s, *params)
        ys, cb_s, s_s = _mixer_layer(ys, state_conv[l], state_hgrn[l], *params)
        conv_p_list.append(cb_p)
        hgrn_p_list.append(s_p)
        conv_s_list.append(cb_s)
        hgrn_s_list.append(s_s)
    new_conv_prompt = jnp.stack(conv_p_list, axis=0)
    new_hgrn_prompt = jnp.stack(hgrn_p_list, axis=0)
    new_conv_sample = jnp.stack(conv_s_list, axis=0)
    new_hgrn_sample = jnp.stack(hgrn_s_list, axis=0)
    return (yp, ys, new_conv_prompt, new_hgrn_prompt, new_conv_sample, new_hgrn_sample)
```

```python
import functools

import jax
import jax.numpy as jnp
from jax import lax
from jax.experimental import pallas as pl
from jax.experimental.pallas import tpu as pltpu

LANES = 128
SUBLANES = 8
CONV_WIDTH = 3
EPS = 1e-5
PROMPT_TILE = 512
PROMPT_CHUNK = 128
ROW_BLOCK = 128
SAMPLE_SEQS = 32
VMEM_LIMIT = 56 * 1024 * 1024

F32 = jnp.float32
BF16 = jnp.bfloat16
_NT = (((1,), (1,)), ((), ()))
_TN = (((0,), (0,)), ((), ()))


def _sigmoid(x):
    return 1.0 / (1.0 + jnp.exp(-x))


def _silu(x):
    return x * _sigmoid(x)


def _lane_rms(x, gain):
    return x * lax.rsqrt(jnp.mean(x * x, axis=-1, keepdims=True) + EPS) * gain


def _lower_bound(lbl):
    e = jnp.exp(lbl - jnp.max(lbl, axis=0, keepdims=True))
    return e[0:1] / jnp.sum(e, axis=0, keepdims=True)


def _segment_cumsum(g, pos, seg):
    s = 1
    while s < seg:
        g = g + jnp.where(pos >= s, pltpu.roll(g, s, axis=0), 0.0)
        s *= 2
    return g


def _layer_norm_rows(y_ref, gain, bias, rows):
    def body(i, _):
        r0 = pl.multiple_of(i * ROW_BLOCK, ROW_BLOCK)
        z = y_ref[pl.ds(r0, ROW_BLOCK), :]
        mu = jnp.mean(z, axis=-1, keepdims=True)
        zc = z - mu
        var = jnp.mean(zc * zc, axis=-1, keepdims=True)
        y_ref[pl.ds(r0, ROW_BLOCK), :] = zc * lax.rsqrt(var + EPS) * gain + bias
        return 0
    lax.fori_loop(0, rows // ROW_BLOCK, body, 0)


def _init_rows(x_ref, xb_ref, y_ref, alpha, rows):
    def body(i, _):
        r0 = pl.multiple_of(i * ROW_BLOCK, ROW_BLOCK)
        x = x_ref[pl.ds(r0, ROW_BLOCK), :]
        xb_ref[pl.ds(r0, ROW_BLOCK), :] = x.astype(BF16)
        y_ref[pl.ds(r0, ROW_BLOCK), :] = alpha * x
        return 0
    lax.fori_loop(0, rows // ROW_BLOCK, body, 0)


def _gates(qb, fb, lb, dk):
    q = _silu(qb) * (dk ** -0.5)
    f = lb + (1.0 - lb) * _sigmoid(fb)
    return q, 1.0 - f, jnp.log(f)


def _prompt_body(x_ref, w_ref, wo_ref, cw_ref, na_ref, lbl_ref, nb_ref, lg_ref, lbias_ref,
                 y_ref, cbuf_ref, snew_ref,
                 xb_ref, p_ref, mix_ref, st_ref, cv_ref, *, tile, chunk, alpha):
    b = pl.program_id(0)
    t = pl.program_id(1)
    j = pl.program_id(2)
    hd = LANES

    @pl.when(j == 0)
    def _():
        _init_rows(x_ref, xb_ref, y_ref, alpha, tile)

    @pl.when(t == 0)
    def _():
        st_ref[j] = jnp.zeros((hd, hd), F32)
        cv_ref[j] = jnp.zeros((SUBLANES, hd), F32)

    p_ref[...] = jnp.dot(xb_ref[...], w_ref[...], preferred_element_type=F32)

    cw = cw_ref[...]
    na = na_ref[...]
    row = lax.broadcasted_iota(jnp.int32, (ROW_BLOCK, hd), 0)

    def conv_block(i, tail):
        r0 = pl.multiple_of(i * ROW_BLOCK, ROW_BLOCK)
        rows = pl.ds(r0, ROW_BLOCK)
        u = p_ref[rows, 2 * hd:3 * hd] * p_ref[rows, 0:hd]
        prev1 = jnp.broadcast_to(tail[SUBLANES - 1:SUBLANES], u.shape)
        prev2 = jnp.broadcast_to(tail[SUBLANES - 2:SUBLANES - 1], u.shape)
        u1 = jnp.where(row == 0, prev1, pltpu.roll(u, 1, axis=0))
        u2 = jnp.where(row == 0, prev2, jnp.where(row == 1, prev1, pltpu.roll(u, 2, axis=0)))
        conv = cw[0:1] * u2 + cw[1:2] * u1 + cw[2:3] * u
        h = p_ref[rows, hd:2 * hd] * conv
        ya = _lane_rms(h, na) * _silu(p_ref[rows, 3 * hd:4 * hd])
        mix_ref[rows, 0:hd] = ya.astype(BF16)
        return u[ROW_BLOCK - SUBLANES:ROW_BLOCK]

    tail = lax.fori_loop(0, tile // ROW_BLOCK, conv_block, cv_ref[j])
    cv_ref[j] = tail
    last_tile = t == pl.num_programs(1) - 1

    @pl.when(last_tile)
    def _():
        cbuf_ref[b, :, pl.ds(pl.multiple_of(j * hd, hd), hd)] = (
            tail[SUBLANES - (CONV_WIDTH - 1):SUBLANES])

    lb = _lower_bound(lbl_ref[...])
    nb = nb_ref[...]
    pos = lax.broadcasted_iota(jnp.int32, (chunk, hd), 0)
    col = lax.broadcasted_iota(jnp.int32, (chunk, chunk), 1)
    causal = lax.broadcasted_iota(jnp.int32, (chunk, chunk), 0) >= col
    mid = chunk // 2 - 1

    def hgrn_chunk(i, _):
        r0 = pl.multiple_of(i * chunk, chunk)
        rows = pl.ds(r0, chunk)
        q, k, g = _gates(p_ref[rows, 4 * hd:5 * hd], p_ref[rows, 5 * hd:6 * hd], lb, hd)
        v = p_ref[rows, 6 * hd:7 * hd]
        b = _segment_cumsum(g, pos, chunk)
        b_mid = b[mid:mid + 1]
        b_last = b[chunk - 1:chunk]
        q_e = q * jnp.exp(b - b_mid)
        k_e = k * jnp.exp(b_mid - b)
        k_t = k * jnp.exp(b_last - b)
        q_i = q_e * jnp.exp(b_mid)
        s_t = st_ref[j]
        scores = lax.dot_general(q_e.astype(BF16), k_e.astype(BF16), _NT,
                                 preferred_element_type=F32)
        probs = jnp.where(causal, scores, 0.0).astype(BF16)
        o = jnp.dot(probs, v.astype(BF16), preferred_element_type=F32)
        o = o + lax.dot_general(q_i.astype(BF16), s_t.astype(BF16), _NT,
                                preferred_element_type=F32)
        st_ref[j] = jnp.exp(b_last) * s_t + jnp.dot(
            v.T.astype(BF16), k_t.astype(BF16), preferred_element_type=F32)
        yb = _lane_rms(o, nb) * _silu(p_ref[rows, 7 * hd:8 * hd])
        mix_ref[rows, hd:2 * hd] = yb.astype(BF16)
        return 0

    lax.fori_loop(0, tile // chunk, hgrn_chunk, 0)

    @pl.when(last_tile)
    def _():
        snew_ref[b, j] = st_ref[j].T

    y_ref[...] += jnp.dot(mix_ref[...], wo_ref[...], preferred_element_type=F32)

    @pl.when(j == pl.num_programs(2) - 1)
    def _():
        _layer_norm_rows(y_ref, lg_ref[...], lbias_ref[...], tile)


def _sample_body(x_ref, hist_ref, s0_ref, w_ref, wo_ref, cw_ref, na_ref, lbl_ref, nb_ref,
                 lg_ref, lbias_ref,
                 y_ref, cbuf_ref, snew_ref,
                 xb_ref, p_ref, mix_ref, qi_ref, kt_ref, v_ref, dl_ref, o_ref,
                 *, seqs, steps, alpha):
    j = pl.program_id(0)
    c = pl.program_id(1)
    hd = LANES
    rows_all = x_ref.shape[0]
    nrow = seqs * steps
    r0 = pl.multiple_of(c * nrow, nrow)
    rows = pl.ds(r0, nrow)

    @pl.when((j == 0) & (c == 0))
    def _():
        _init_rows(x_ref, xb_ref, y_ref, alpha, rows_all)

    @pl.when(c == 0)
    def _():
        p_ref[...] = jnp.dot(xb_ref[...], w_ref[...], preferred_element_type=F32)

    pos = lax.broadcasted_iota(jnp.int32, (nrow, hd), 0) % steps

    cw = cw_ref[...]
    hist = hist_ref[...]
    u = p_ref[rows, 2 * hd:3 * hd] * p_ref[rows, 0:hd]
    u1 = jnp.where(pos == 0, pltpu.roll(hist, nrow - 1, axis=0), pltpu.roll(u, 1, axis=0))
    u2 = jnp.where(pos < 2, hist, pltpu.roll(u, 2, axis=0))
    conv = cw[0:1] * u2 + cw[1:2] * u1 + cw[2:3] * u
    h = p_ref[rows, hd:2 * hd] * conv
    ya = _lane_rms(h, na_ref[...]) * _silu(p_ref[rows, 3 * hd:4 * hd])
    mix_ref[rows, 0:hd] = ya.astype(BF16)
    cbuf_ref[...] = u.reshape(seqs, steps, hd)[:, steps - (CONV_WIDTH - 1):, :]

    lb = _lower_bound(lbl_ref[...])
    q, k, g = _gates(p_ref[rows, 4 * hd:5 * hd], p_ref[rows, 5 * hd:6 * hd], lb, hd)
    v = p_ref[rows, 6 * hd:7 * hd]
    b = _segment_cumsum(g, pos, steps)
    b3 = b.reshape(seqs, steps, hd)
    b_last = jnp.broadcast_to(b3[:, steps - 1:steps, :], b3.shape).reshape(nrow, hd)
    q_e = q * jnp.exp(b)
    k_e = k * jnp.exp(-b)
    qi_ref[...] = q_e
    kt_ref[...] = k * jnp.exp(b_last - b)
    v_ref[...] = v
    dl_ref[...] = jnp.exp(b_last)

    ri = lax.broadcasted_iota(jnp.int32, (nrow, nrow), 0)
    ci = lax.broadcasted_iota(jnp.int32, (nrow, nrow), 1)
    same_causal = (ri // steps == ci // steps) & (ri >= ci)
    scores = lax.dot_general(q_e.astype(BF16), k_e.astype(BF16), _NT, preferred_element_type=F32)
    probs = jnp.where(same_causal, scores, 0.0).astype(BF16)
    o_ref[...] = jnp.dot(probs, v.astype(BF16), preferred_element_type=F32)

    def seq_body(s, _):
        sr = pl.ds(pl.multiple_of(s * steps, steps), steps)
        s0 = s0_ref[s]
        o_ref[sr, :] += jnp.dot(qi_ref[sr, :].astype(BF16), s0.astype(BF16),
                                preferred_element_type=F32)
        upd = lax.dot_general(kt_ref[sr, :].astype(BF16), v_ref[sr, :].astype(BF16), _TN,
                              preferred_element_type=F32)
        decay = jnp.broadcast_to(dl_ref[pl.ds(s * steps, 1), :], (hd, hd)).T
        snew_ref[s] = decay * s0 + upd
        return 0

    lax.fori_loop(0, seqs, seq_body, 0)

    yb = _lane_rms(o_ref[...], nb_ref[...]) * _silu(p_ref[rows, 7 * hd:8 * hd])
    mix_ref[rows, hd:2 * hd] = yb.astype(BF16)

    last_c = c == pl.num_programs(1) - 1

    @pl.when(last_c)
    def _():
        y_ref[...] += jnp.dot(mix_ref[...], wo_ref[...], preferred_element_type=F32)

    @pl.when(last_c & (j == pl.num_programs(0) - 1))
    def _():
        _layer_norm_rows(y_ref, lg_ref[...], lbias_ref[...], rows_all)


def kernel(x_prompt, x_sample, state_conv, state_hgrn, w_in, conv_w, norm_a, lb_logits,
           norm_b, w_out, ln_gain, ln_bias):
    batch, seq, d_model = x_prompt.shape
    dec_batch, dec_seq, _ = x_sample.shape
    depth, _, n_proj = w_in.shape
    assert depth == 1
    heads, dk, dv = state_hgrn.shape[2:]
    w_conv = state_conv.shape[-1]
    hd = LANES
    assert dk == hd and dv == hd and w_conv == heads * hd and n_proj == 8 * heads * hd
    assert dec_seq == SUBLANES and seq % PROMPT_TILE == 0 and dec_batch % SAMPLE_SEQS == 0
    alpha = (2.0 * depth) ** 0.25
    f32 = x_prompt.dtype

    w_r = w_in[0].reshape(d_model, 8, heads, hd).transpose(2, 0, 1, 3)
    w_r = w_r.reshape(heads, d_model, 8 * hd).astype(BF16)
    wo_r = w_out[0].reshape(2, heads, hd, d_model).transpose(1, 0, 2, 3)
    wo_r = wo_r.reshape(heads, 2 * hd, d_model).astype(BF16)

    w_spec = lambda im: pl.BlockSpec((None, d_model, 8 * hd), im)
    wo_spec = lambda im: pl.BlockSpec((None, 2 * hd, d_model), im)

    def strip(arr_rows, head_of):
        return pl.BlockSpec((arr_rows, hd), lambda *g: (0, head_of(*g)))

    tile = PROMPT_TILE
    pj = lambda b, t, j: j
    y_p, conv_p, hgrn_p = pl.pallas_call(
        functools.partial(_prompt_body, tile=tile, chunk=PROMPT_CHUNK, alpha=alpha),
        grid=(batch, seq // tile, heads),
        in_specs=[
            pl.BlockSpec((None, tile, d_model), lambda b, t, j: (b, t, 0)),
            w_spec(lambda b, t, j: (j, 0, 0)),
            wo_spec(lambda b, t, j: (j, 0, 0)),
            pl.BlockSpec((None, CONV_WIDTH, hd), lambda b, t, j: (0, 0, j)),
            strip(1, pj), strip(lb_logits.shape[0], pj), strip(1, pj),
            pl.BlockSpec((1, d_model), lambda b, t, j: (0, 0)),
            pl.BlockSpec((1, d_model), lambda b, t, j: (0, 0)),
        ],
        out_specs=[
            pl.BlockSpec((None, tile, d_model), lambda b, t, j: (b, t, 0)),
            pl.BlockSpec((None, batch, CONV_WIDTH - 1, w_conv), lambda b, t, j: (0, 0, 0, 0)),
            pl.BlockSpec((None, batch, heads, hd, hd), lambda b, t, j: (0, 0, 0, 0, 0)),
        ],
        out_shape=[
            jax.ShapeDtypeStruct((batch, seq, d_model), f32),
            jax.ShapeDtypeStruct((depth, batch, CONV_WIDTH - 1, w_conv), state_conv.dtype),
            jax.ShapeDtypeStruct((depth, batch, heads, dk, dv), state_hgrn.dtype),
        ],
        scratch_shapes=[
            pltpu.VMEM((tile, d_model), BF16),
            pltpu.VMEM((tile, 8 * hd), F32),
            pltpu.VMEM((tile, 2 * hd), BF16),
            pltpu.VMEM((heads, hd, hd), F32),
            pltpu.VMEM((heads, SUBLANES, hd), F32),
        ],
        compiler_params=pltpu.CompilerParams(
            dimension_semantics=("arbitrary", "arbitrary", "arbitrary"),
            vmem_limit_bytes=VMEM_LIMIT),
        name="prompt_mixer",
    )(x_prompt, w_r, wo_r, conv_w, norm_a, lb_logits, norm_b, ln_gain, ln_bias)

    rows_s = dec_batch * dec_seq
    nrow = SAMPLE_SEQS * dec_seq
    xs = x_sample.reshape(rows_s, d_model)
    hist = jnp.pad(state_conv[0], ((0, 0), (0, dec_seq - (CONV_WIDTH - 1)), (0, 0)))
    hist = hist.reshape(rows_s, w_conv)
    sj = lambda j, c: j
    resident = lambda shape: pl.BlockSpec(shape, lambda j, c: (0, 0),
                                          pipeline_mode=pl.Buffered(1))
    y_s, conv_s, hgrn_s = pl.pallas_call(
        functools.partial(_sample_body, seqs=SAMPLE_SEQS, steps=dec_seq, alpha=alpha),
        grid=(heads, dec_batch // SAMPLE_SEQS),
        in_specs=[
            resident((rows_s, d_model)),
            pl.BlockSpec((nrow, hd), lambda j, c: (c, j)),
            pl.BlockSpec((None, SAMPLE_SEQS, None, hd, hd), lambda j, c: (0, c, j, 0, 0)),
            w_spec(lambda j, c: (j, 0, 0)),
            wo_spec(lambda j, c: (j, 0, 0)),
            pl.BlockSpec((None, CONV_WIDTH, hd), lambda j, c: (0, 0, j)),
            strip(1, sj), strip(lb_logits.shape[0], sj), strip(1, sj),
            pl.BlockSpec((1, d_model), lambda j, c: (0, 0)),
            pl.BlockSpec((1, d_model), lambda j, c: (0, 0)),
        ],
        out_specs=[
            resident((rows_s, d_model)),
            pl.BlockSpec((None, SAMPLE_SEQS, CONV_WIDTH - 1, hd), lambda j, c: (0, c, 0, j)),
            pl.BlockSpec((None, SAMPLE_SEQS, None, hd, hd), lambda j, c: (0, c, j, 0, 0)),
        ],
        out_shape=[
            jax.ShapeDtypeStruct((rows_s, d_model), f32),
            jax.ShapeDtypeStruct((depth, dec_batch, CONV_WIDTH - 1, w_conv), state_conv.dtype),
            jax.ShapeDtypeStruct((depth, dec_batch, heads, dk, dv), state_hgrn.dtype),
        ],
        scratch_shapes=[
            pltpu.VMEM((rows_s, d_model), BF16),
            pltpu.VMEM((rows_s, 8 * hd), F32),
            pltpu.VMEM((rows_s, 2 * hd), BF16),
            pltpu.VMEM((nrow, hd), F32),
            pltpu.VMEM((nrow, hd), F32),
            pltpu.VMEM((nrow, hd), F32),
            pltpu.VMEM((nrow, hd), F32),
            pltpu.VMEM((nrow, hd), F32),
        ],
        compiler_params=pltpu.CompilerParams(
            dimension_semantics=("arbitrary", "arbitrary"),
            vmem_limit_bytes=VMEM_LIMIT),
        name="sample_mixer",
    )(xs, hist, state_hgrn, w_r, wo_r, conv_w, norm_a, lb_logits, norm_b, ln_gain, ln_bias)

    return (y_p, y_s.reshape(x_sample.shape), conv_p, hgrn_p, conv_s, hgrn_s)
```

```python
import functools

import jax
import jax.numpy as jnp
from jax import lax
from jax.experimental import pallas as pl
from jax.experimental.pallas import tpu as pltpu

LANES = 128
SUBLANES = 8
MXU_COLS = 256
STRIPS = 8
COL_BLOCKS = STRIPS * LANES // MXU_COLS
CONV_WIDTH = 3
EPS = 1e-5
PROMPT_TILE = 512
CHUNK = 128
SAMPLE_SEQS = 32
VMEM_LIMIT = 56 * 1024 * 1024

F32 = jnp.float32
BF16 = jnp.bfloat16
_NT = (((1,), (1,)), ((), ()))
_TN = (((0,), (0,)), ((), ()))


def _sigmoid(x):
    return 1.0 / (1.0 + jnp.exp(-x))


def _silu(x):
    return x * _sigmoid(x)


def _lane_rms(x, gain):
    return x * lax.rsqrt(jnp.mean(x * x, axis=-1, keepdims=True) + EPS) * gain


def _lower_bound(lbl):
    e = jnp.exp(lbl - jnp.max(lbl, axis=0, keepdims=True))
    return e[0:1] / jnp.sum(e, axis=0, keepdims=True)


def _segment_cumsum(g, pos, seg):
    s = 1
    while s < seg:
        g = g + jnp.where(pos >= s, pltpu.roll(g, s, axis=0), 0.0)
        s *= 2
    return g


def _layer_norm_rows(y_ref, gain, bias, rows):
    def body(i, _):
        r0 = pl.multiple_of(i * CHUNK, CHUNK)
        z = y_ref[pl.ds(r0, CHUNK), :]
        mu = jnp.mean(z, axis=-1, keepdims=True)
        zc = z - mu
        var = jnp.mean(zc * zc, axis=-1, keepdims=True)
        y_ref[pl.ds(r0, CHUNK), :] = zc * lax.rsqrt(var + EPS) * gain + bias
        return 0
    lax.fori_loop(0, rows // CHUNK, body, 0)


def _init_rows(x_ref, xb_ref, y_ref, alpha, rows):
    def body(i, _):
        r0 = pl.multiple_of(i * CHUNK, CHUNK)
        x = x_ref[pl.ds(r0, CHUNK), :]
        xb_ref[pl.ds(r0, CHUNK), :] = x.astype(BF16)
        y_ref[pl.ds(r0, CHUNK), :] = alpha * x
        return 0
    lax.fori_loop(0, rows // CHUNK, body, 0)


def _gates(qb, fb, lb, dk):
    q = _silu(qb) * (dk ** -0.5)
    f = lb + (1.0 - lb) * _sigmoid(fb)
    return q, 1.0 - f, jnp.log(f)


def _strip_reader(p_ref):
    def strip(k, rows):
        lo = (k % 2) * LANES
        return p_ref[k // 2, rows, lo:lo + LANES]
    return strip


def _by_parity(jj, stage, p_even, p_odd):
    @pl.when(jj % 2 == 0)
    def _():
        stage(p_even, p_odd)

    @pl.when(jj % 2 == 1)
    def _():
        stage(p_odd, p_even)


def _conv_apply(u, u1, u2, cw, b_a, z_a, na):
    conv = cw[0:1] * u2 + cw[1:2] * u1 + cw[2:3] * u
    return _lane_rms(b_a * conv, na) * _silu(z_a)


def _relayout_body(*refs):
    strips, (wo_a, wo_b, w_o, wo_o) = refs[:STRIPS], refs[STRIPS:]
    for k in range(STRIPS):
        lo = (k % 2) * LANES
        w_o[k // 2, :, lo:lo + LANES] = strips[k][...].astype(BF16)
    wo_o[0:LANES, :] = wo_a[...].astype(BF16)
    wo_o[LANES:2 * LANES, :] = wo_b[...].astype(BF16)


def _relayout_weights(w_in, w_out, heads):
    d_model = w_in.shape[0]
    hd = LANES
    strip_spec = lambda k: pl.BlockSpec((d_model, hd), lambda j: (0, k * heads + j))
    return pl.pallas_call(
        _relayout_body,
        grid=(heads,),
        in_specs=[strip_spec(k) for k in range(STRIPS)] + [
            pl.BlockSpec((hd, d_model), lambda j: (j, 0)),
            pl.BlockSpec((hd, d_model), lambda j: (heads + j, 0)),
        ],
        out_specs=[
            pl.BlockSpec((None, COL_BLOCKS, d_model, MXU_COLS), lambda j: (j, 0, 0, 0)),
            pl.BlockSpec((None, 2 * hd, d_model), lambda j: (j, 0, 0)),
        ],
        out_shape=[
            jax.ShapeDtypeStruct((heads, COL_BLOCKS, d_model, MXU_COLS), BF16),
            jax.ShapeDtypeStruct((heads, 2 * hd, d_model), BF16),
        ],
        compiler_params=pltpu.CompilerParams(
            dimension_semantics=("arbitrary",), vmem_limit_bytes=VMEM_LIMIT),
        name="relayout_weights",
    )(*([w_in] * STRIPS), w_out, w_out)


def _prompt_body(x_ref, w_ref, wo_ref, cw_ref, na_ref, lbl_ref, nb_ref, lg_ref, lbias_ref,
                 y_ref, cbuf_ref, snew_ref,
                 xb_ref, p_even, p_odd, mix_ref, st_ref, cv_ref, *, tile, alpha):
    b = pl.program_id(0)
    t = pl.program_id(1)
    jj = pl.program_id(2)
    pairs = pl.num_programs(2) - 1
    j = jnp.maximum(jj - 1, 0)
    hd = LANES
    chunks_per_step = tile // (COL_BLOCKS * CHUNK)

    def project(p_write, i):
        p_write[i] = jnp.dot(xb_ref[...], w_ref[i], preferred_element_type=F32)

    @pl.when(jj == 0)
    def _():
        _init_rows(x_ref, xb_ref, y_ref, alpha, tile)

        @pl.when(t == 0)
        def _():
            st_ref[...] = jnp.zeros_like(st_ref)
            cv_ref[...] = jnp.zeros_like(cv_ref)

        def only_project(i, _):
            project(p_even, i)
            return 0
        lax.fori_loop(0, COL_BLOCKS, only_project, 0)

    def stage(p_write, p_read):
        strip = _strip_reader(p_read)
        cw = cw_ref[...]
        na = na_ref[...]
        nb = nb_ref[...]
        lb = _lower_bound(lbl_ref[...])
        pos = lax.broadcasted_iota(jnp.int32, (CHUNK, hd), 0)
        causal = (lax.broadcasted_iota(jnp.int32, (CHUNK, CHUNK), 0)
                  >= lax.broadcasted_iota(jnp.int32, (CHUNK, CHUNK), 1))
        mid = CHUNK // 2 - 1

        def conv_chunk(rows, tail):
            u = strip(2, rows) * strip(0, rows)
            prev1 = jnp.broadcast_to(tail[SUBLANES - 1:SUBLANES], u.shape)
            prev2 = jnp.broadcast_to(tail[SUBLANES - 2:SUBLANES - 1], u.shape)
            u1 = jnp.where(pos == 0, prev1, pltpu.roll(u, 1, axis=0))
            u2 = jnp.where(pos == 0, prev2, jnp.where(pos == 1, prev1, pltpu.roll(u, 2, axis=0)))
            ya = _conv_apply(u, u1, u2, cw, strip(1, rows), strip(3, rows), na)
            mix_ref[rows, 0:hd] = ya.astype(BF16)
            return u[CHUNK - SUBLANES:CHUNK]

        def hgrn_chunk(rows):
            q, k, g = _gates(strip(4, rows), strip(5, rows), lb, hd)
            v = strip(6, rows)
            cum = _segment_cumsum(g, pos, CHUNK)
            c_mid = cum[mid:mid + 1]
            c_last = cum[CHUNK - 1:CHUNK]
            q_e = q * jnp.exp(cum - c_mid)
            k_e = k * jnp.exp(c_mid - cum)
            k_t = k * jnp.exp(c_last - cum)
            q_i = q_e * jnp.exp(c_mid)
            s_t = st_ref[j]
            scores = lax.dot_general(q_e.astype(BF16), k_e.astype(BF16), _NT,
                                     preferred_element_type=F32)
            probs = jnp.where(causal, scores, 0.0).astype(BF16)
            o = jnp.dot(probs, v.astype(BF16), preferred_element_type=F32)
            o = o + lax.dot_general(q_i.astype(BF16), s_t.astype(BF16), _NT,
                                    preferred_element_type=F32)
            st_ref[j] = jnp.exp(c_last) * s_t + jnp.dot(
                v.T.astype(BF16), k_t.astype(BF16), preferred_element_type=F32)
            yb = _lane_rms(o, nb) * _silu(strip(7, rows))
            mix_ref[rows, hd:2 * hd] = yb.astype(BF16)

        def step(i, tail):
            project(p_write, i)
            for c in range(chunks_per_step):
                r0 = pl.multiple_of((i * chunks_per_step + c) * CHUNK, CHUNK)
                rows = pl.ds(r0, CHUNK)
                tail = conv_chunk(rows, tail)
                hgrn_chunk(rows)
            return tail

        tail = cv_ref[j]
        for i in range(COL_BLOCKS):
            tail = step(i, tail)
        cv_ref[j] = tail

        @pl.when(t == pl.num_programs(1) - 1)
        def _():
            cbuf_ref[b, :, pl.ds(pl.multiple_of(j * hd, hd), hd)] = (
                tail[SUBLANES - (CONV_WIDTH - 1):SUBLANES])
            snew_ref[b, j] = st_ref[j].T

        y_ref[...] += jnp.dot(mix_ref[...], wo_ref[...], preferred_element_type=F32)

    @pl.when(jj > 0)
    def _():
        _by_parity(jj, stage, p_even, p_odd)

    @pl.when(jj == pairs)
    def _():
        _layer_norm_rows(y_ref, lg_ref[...], lbias_ref[...], tile)


def _sample_body(x_ref, hist_ref, s0_ref, w_ref, wo_ref, cw_ref, na_ref, lbl_ref, nb_ref,
                 lg_ref, lbias_ref,
                 y_ref, cbuf_ref, snew_ref,
                 xb_ref, p_even, p_odd, mix_ref, *, seqs, steps, alpha):
    jj = pl.program_id(0)
    c = pl.program_id(1)
    pairs = pl.num_programs(0) - 1
    hd = LANES
    rows_all = x_ref.shape[0]
    nrow = seqs * steps

    def project(p_write):
        p_write[c] = jnp.dot(xb_ref[...], w_ref[c], preferred_element_type=F32)

    @pl.when(jj == 0)
    def _():
        @pl.when(c == 0)
        def _():
            _init_rows(x_ref, xb_ref, y_ref, alpha, rows_all)
        project(p_even)

    def stage(p_write, p_read):
        project(p_write)
        rows = pl.ds(pl.multiple_of(c * nrow, nrow), nrow)
        strip = _strip_reader(p_read)
        pos = lax.broadcasted_iota(jnp.int32, (nrow, hd), 0) % steps
        per_row = lambda a: jnp.broadcast_to(a, (seqs, steps, hd)).reshape(nrow, hd)

        hist = hist_ref[...]
        prev2 = per_row(hist[:, 0:1, :])
        prev1 = per_row(hist[:, 1:2, :])
        u = strip(2, rows) * strip(0, rows)
        u1 = jnp.where(pos == 0, prev1, pltpu.roll(u, 1, axis=0))
        u2 = jnp.where(pos == 0, prev2, jnp.where(pos == 1, prev1, pltpu.roll(u, 2, axis=0)))
        ya = _conv_apply(u, u1, u2, cw_ref[...], strip(1, rows), strip(3, rows), na_ref[...])
        mix_ref[rows, 0:hd] = ya.astype(BF16)
        cbuf_ref[...] = u.reshape(seqs, steps, hd)[:, steps - (CONV_WIDTH - 1):, :]

        lb = _lower_bound(lbl_ref[...])
        q, k, g = _gates(strip(4, rows), strip(5, rows), lb, hd)
        v = strip(6, rows)
        cum = _segment_cumsum(g, pos, steps)
        c_last = per_row(cum.reshape(seqs, steps, hd)[:, steps - 1:steps, :])
        q_e = q * jnp.exp(cum)
        k_e = k * jnp.exp(-cum)
        k_t = k * jnp.exp(c_last - cum)
        v_b = v.astype(BF16)
        decay = jnp.exp(c_last)

        ri = lax.broadcasted_iota(jnp.int32, (nrow, nrow), 0)
        ci = lax.broadcasted_iota(jnp.int32, (nrow, nrow), 1)
        same_causal = (ri // steps == ci // steps) & (ri >= ci)
        scores = lax.dot_general(q_e.astype(BF16), k_e.astype(BF16), _NT,
                                 preferred_element_type=F32)
        probs = jnp.where(same_causal, scores, 0.0).astype(BF16)
        o_intra = jnp.dot(probs, v_b, preferred_element_type=F32)

        o_inter = []
        for s in range(seqs):
            sr = slice(s * steps, (s + 1) * steps)
            s0 = s0_ref[s]
            o_inter.append(jnp.dot(q_e[sr].astype(BF16), s0.astype(BF16),
                                   preferred_element_type=F32))
            upd = lax.dot_general(k_t[sr].astype(BF16), v[sr].astype(BF16), _TN,
                                  preferred_element_type=F32)
            decay_col = jnp.broadcast_to(decay[s * steps:s * steps + 1], (hd, hd)).T
            snew_ref[s] = decay_col * s0 + upd
        o = o_intra + jnp.concatenate(o_inter, axis=0)

        yb = _lane_rms(o, nb_ref[...]) * _silu(strip(7, rows))
        mix_ref[rows, hd:2 * hd] = yb.astype(BF16)

    @pl.when(jj > 0)
    def _():
        _by_parity(jj, stage, p_even, p_odd)

        @pl.when(c == pl.num_programs(1) - 1)
        def _():
            y_ref[...] += jnp.dot(mix_ref[...], wo_ref[...], preferred_element_type=F32)

            @pl.when(jj == pairs)
            def _():
                _layer_norm_rows(y_ref, lg_ref[...], lbias_ref[...], rows_all)


def kernel(x_prompt, x_sample, state_conv, state_hgrn, w_in, conv_w, norm_a, lb_logits,
           norm_b, w_out, ln_gain, ln_bias):
    batch, seq, d_model = x_prompt.shape
    dec_batch, dec_seq, _ = x_sample.shape
    depth, _, n_proj = w_in.shape
    assert depth == 1
    heads, dk, dv = state_hgrn.shape[2:]
    w_conv = state_conv.shape[-1]
    hd = LANES
    assert dk == hd and dv == hd and w_conv == heads * hd and n_proj == STRIPS * heads * hd
    assert dec_seq == SUBLANES and dec_batch % SAMPLE_SEQS == 0
    assert seq % PROMPT_TILE == 0 and PROMPT_TILE % (COL_BLOCKS * CHUNK) == 0
    assert dec_batch // SAMPLE_SEQS == COL_BLOCKS
    alpha = (2.0 * depth) ** 0.25
    f32 = x_prompt.dtype

    w_r, wo_r = _relayout_weights(w_in[0], w_out[0], heads)

    last = heads - 1
    w_spec = lambda im: pl.BlockSpec((None, COL_BLOCKS, d_model, MXU_COLS), im)
    wo_spec = lambda im: pl.BlockSpec((None, 2 * hd, d_model), im)

    def strip_spec(arr_rows, pair_of):
        return pl.BlockSpec((arr_rows, hd), lambda *g: (0, pair_of(*g)))

    tile = PROMPT_TILE
    mm_p = lambda b, t, jj: jnp.minimum(jj, last)
    ew_p = lambda b, t, jj: jnp.maximum(jj - 1, 0)
    y_p, conv_p, hgrn_p = pl.pallas_call(
        functools.partial(_prompt_body, tile=tile, alpha=alpha),
        grid=(batch, seq // tile, heads + 1),
        in_specs=[
            pl.BlockSpec((None, tile, d_model), lambda b, t, jj: (b, t, 0)),
            w_spec(lambda b, t, jj: (mm_p(b, t, jj), 0, 0, 0)),
            wo_spec(lambda b, t, jj: (ew_p(b, t, jj), 0, 0)),
            pl.BlockSpec((None, CONV_WIDTH, hd), lambda b, t, jj: (0, 0, ew_p(b, t, jj))),
            strip_spec(1, ew_p), strip_spec(lb_logits.shape[0], ew_p), strip_spec(1, ew_p),
            pl.BlockSpec((1, d_model), lambda b, t, jj: (0, 0)),
            pl.BlockSpec((1, d_model), lambda b, t, jj: (0, 0)),
        ],
        out_specs=[
            pl.BlockSpec((None, tile, d_model), lambda b, t, jj: (b, t, 0)),
            pl.BlockSpec((None, batch, CONV_WIDTH - 1, w_conv), lambda b, t, jj: (0, 0, 0, 0)),
            pl.BlockSpec((None, batch, heads, hd, hd), lambda b, t, jj: (0, 0, 0, 0, 0)),
        ],
        out_shape=[
            jax.ShapeDtypeStruct((batch, seq, d_model), f32),
            jax.ShapeDtypeStruct((depth, batch, CONV_WIDTH - 1, w_conv), state_conv.dtype),
            jax.ShapeDtypeStruct((depth, batch, heads, dk, dv), state_hgrn.dtype),
        ],
        scratch_shapes=[
            pltpu.VMEM((tile, d_model), BF16),
            pltpu.VMEM((COL_BLOCKS, tile, MXU_COLS), F32),
            pltpu.VMEM((COL_BLOCKS, tile, MXU_COLS), F32),
            pltpu.VMEM((tile, 2 * hd), BF16),
            pltpu.VMEM((heads, hd, hd), F32),
            pltpu.VMEM((heads, SUBLANES, hd), F32),
        ],
        compiler_params=pltpu.CompilerParams(
            dimension_semantics=("arbitrary", "arbitrary", "arbitrary"),
            vmem_limit_bytes=VMEM_LIMIT),
        name="prompt_mixer",
    )(x_prompt, w_r, wo_r, conv_w, norm_a, lb_logits, norm_b, ln_gain, ln_bias)

    rows_s = dec_batch * dec_seq
    xs = x_sample.reshape(rows_s, d_model)
    mm_s = lambda jj, c: jnp.minimum(jj, last)
    ew_s = lambda jj, c: jnp.maximum(jj - 1, 0)
    resident = lambda shape: pl.BlockSpec(shape, lambda jj, c: (0, 0),
                                          pipeline_mode=pl.Buffered(1))
    ch_s = lambda jj, c: jnp.where(jj == 0, 0, c)
    state_spec = pl.BlockSpec((None, SAMPLE_SEQS, None, hd, hd),
                              lambda jj, c: (0, ch_s(jj, c), ew_s(jj, c), 0, 0))
    hist_spec = pl.BlockSpec((None, SAMPLE_SEQS, CONV_WIDTH - 1, hd),
                             lambda jj, c: (0, ch_s(jj, c), 0, ew_s(jj, c)))
    y_s, conv_s, hgrn_s = pl.pallas_call(
        functools.partial(_sample_body, seqs=SAMPLE_SEQS, steps=dec_seq, alpha=alpha),
        grid=(heads + 1, dec_batch // SAMPLE_SEQS),
        in_specs=[
            resident((rows_s, d_model)),
            hist_spec,
            state_spec,
            w_spec(lambda jj, c: (mm_s(jj, c), 0, 0, 0)),
            wo_spec(lambda jj, c: (ew_s(jj, c), 0, 0)),
            pl.BlockSpec((None, CONV_WIDTH, hd), lambda jj, c: (0, 0, ew_s(jj, c))),
            strip_spec(1, ew_s), strip_spec(lb_logits.shape[0], ew_s), strip_spec(1, ew_s),
            pl.BlockSpec((1, d_model), lambda jj, c: (0, 0)),
            pl.BlockSpec((1, d_model), lambda jj, c: (0, 0)),
        ],
        out_specs=[resident((rows_s, d_model)), hist_spec, state_spec],
        out_shape=[
            jax.ShapeDtypeStruct((rows_s, d_model), f32),
            jax.ShapeDtypeStruct((depth, dec_batch, CONV_WIDTH - 1, w_conv), state_conv.dtype),
            jax.ShapeDtypeStruct((depth, dec_batch, heads, dk, dv), state_hgrn.dtype),
        ],
        scratch_shapes=[
            pltpu.VMEM((rows_s, d_model), BF16),
            pltpu.VMEM((COL_BLOCKS, rows_s, MXU_COLS), F32),
            pltpu.VMEM((COL_BLOCKS, rows_s, MXU_COLS), F32),
            pltpu.VMEM((rows_s, 2 * hd), BF16),
        ],
        compiler_params=pltpu.CompilerParams(
            dimension_semantics=("arbitrary", "arbitrary"),
            vmem_limit_bytes=VMEM_LIMIT),
        name="sample_mixer",
    )(xs, state_conv, state_hgrn, w_r, wo_r, conv_w, norm_a, lb_logits, norm_b, ln_gain, ln_bias)

    return (y_p, y_s.reshape(x_sample.shape), conv_p, hgrn_p, conv_s, hgrn_s)
```

```python
import functools

import jax
import jax.numpy as jnp
from jax import lax
from jax.experimental import pallas as pl
from jax.experimental.pallas import tpu as pltpu

LANES = 128
SUBLANES = 8
MXU_COLS = 256
STRIPS = 8
COL_BLOCKS = STRIPS * LANES // MXU_COLS
CONV_WIDTH = 3
EPS = 1e-5
PROMPT_TILE = 512
CHUNK = 128
SAMPLE_SEQS = 32
VMEM_LIMIT = 56 * 1024 * 1024

F32 = jnp.float32
BF16 = jnp.bfloat16
_NT = (((1,), (1,)), ((), ()))
_TN = (((0,), (0,)), ((), ()))


def _sigmoid(x):
    return 1.0 / (1.0 + jnp.exp(-x))


def _silu(x):
    return x * _sigmoid(x)


def _lane_rms(x, gain):
    return x * lax.rsqrt(jnp.mean(x * x, axis=-1, keepdims=True) + EPS) * gain


def _lower_bound(lbl):
    e = jnp.exp(lbl - jnp.max(lbl, axis=0, keepdims=True))
    return e[0:1] / jnp.sum(e, axis=0, keepdims=True)


def _segment_cumsum(g, pos, seg):
    s = 1
    while s < seg:
        g = g + jnp.where(pos >= s, pltpu.roll(g, s, axis=0), 0.0)
        s *= 2
    return g


def _layer_norm_rows(y_ref, gain, bias, rows):
    def body(i, _):
        r0 = pl.multiple_of(i * CHUNK, CHUNK)
        z = y_ref[pl.ds(r0, CHUNK), :]
        mu = jnp.mean(z, axis=-1, keepdims=True)
        zc = z - mu
        var = jnp.mean(zc * zc, axis=-1, keepdims=True)
        y_ref[pl.ds(r0, CHUNK), :] = zc * lax.rsqrt(var + EPS) * gain + bias
        return 0
    lax.fori_loop(0, rows // CHUNK, body, 0)


def _init_rows(x_ref, xb_ref, y_ref, alpha, rows):
    def body(i, _):
        r0 = pl.multiple_of(i * CHUNK, CHUNK)
        x = x_ref[pl.ds(r0, CHUNK), :]
        xb_ref[pl.ds(r0, CHUNK), :] = x.astype(BF16)
        y_ref[pl.ds(r0, CHUNK), :] = alpha * x
        return 0
    lax.fori_loop(0, rows // CHUNK, body, 0)


def _cast_rows(x_ref, xb_ref, rows):
    def body(i, _):
        r0 = pl.multiple_of(i * CHUNK, CHUNK)
        xb_ref[pl.ds(r0, CHUNK), :] = x_ref[pl.ds(r0, CHUNK), :].astype(BF16)
        return 0
    lax.fori_loop(0, rows // CHUNK, body, 0)


def _scale_rows(x_ref, y_ref, alpha, rows):
    def body(i, _):
        r0 = pl.multiple_of(i * CHUNK, CHUNK)
        y_ref[pl.ds(r0, CHUNK), :] = alpha * x_ref[pl.ds(r0, CHUNK), :]
        return 0
    lax.fori_loop(0, rows // CHUNK, body, 0)


def _gates(qb, fb, lb, dk):
    q = _silu(qb) * (dk ** -0.5)
    f = lb + (1.0 - lb) * _sigmoid(fb)
    return q, 1.0 - f, jnp.log(f)


def _strip_reader(p_ref):
    def strip(k, rows):
        lo = (k % 2) * LANES
        return p_ref[k // 2, rows, lo:lo + LANES]
    return strip


def _by_parity(jj, stage, p_even, p_odd):
    @pl.when(jj % 2 == 0)
    def _():
        stage(p_even, p_odd)

    @pl.when(jj % 2 == 1)
    def _():
        stage(p_odd, p_even)


def _conv_apply(u, u1, u2, cw, b_a, z_a, na):
    conv = cw[0:1] * u2 + cw[1:2] * u1 + cw[2:3] * u
    return _lane_rms(b_a * conv, na) * _silu(z_a)


def _relayout_body(*refs):
    strips, (wo_a, wo_b, w_o, wo_o) = refs[:STRIPS], refs[STRIPS:]
    for k in range(STRIPS):
        lo = (k % 2) * LANES
        w_o[k // 2, :, lo:lo + LANES] = strips[k][...].astype(BF16)
    wo_o[0:LANES, :] = wo_a[...].astype(BF16)
    wo_o[LANES:2 * LANES, :] = wo_b[...].astype(BF16)


def _relayout_weights(w_in, w_out, heads):
    d_model = w_in.shape[0]
    hd = LANES
    strip_spec = lambda k: pl.BlockSpec((d_model, hd), lambda j: (0, k * heads + j))
    return pl.pallas_call(
        _relayout_body,
        grid=(heads,),
        in_specs=[strip_spec(k) for k in range(STRIPS)] + [
            pl.BlockSpec((hd, d_model), lambda j: (j, 0)),
            pl.BlockSpec((hd, d_model), lambda j: (heads + j, 0)),
        ],
        out_specs=[
            pl.BlockSpec((None, COL_BLOCKS, d_model, MXU_COLS), lambda j: (j, 0, 0, 0)),
            pl.BlockSpec((None, 2 * hd, d_model), lambda j: (j, 0, 0)),
        ],
        out_shape=[
            jax.ShapeDtypeStruct((heads, COL_BLOCKS, d_model, MXU_COLS), BF16),
            jax.ShapeDtypeStruct((heads, 2 * hd, d_model), BF16),
        ],
        compiler_params=pltpu.CompilerParams(
            dimension_semantics=("arbitrary",), vmem_limit_bytes=VMEM_LIMIT),
        name="relayout_weights",
    )(*([w_in] * STRIPS), w_out, w_out)


def _prompt_body(x_ref, w_ref, wo_ref, cw_ref, na_ref, lbl_ref, nb_ref, lg_ref, lbias_ref,
                 y_ref, cbuf_ref, snew_ref,
                 xb_ref, p_even, p_odd, mix_even, mix_odd, st_ref, cv_ref,
                 *, tile, tiles_per_seq, pairs, units, alpha):
    n = pl.program_id(0)
    hd = LANES
    chunks_per_block = tile // (COL_BLOCKS * CHUNK)
    unit_e = jnp.clip(n - 1, 0, units - 1)
    j = unit_e % pairs
    row_tile = unit_e // pairs
    b = row_tile // tiles_per_seq
    t = row_tile % tiles_per_seq

    @pl.when((n % pairs == 0) & (n < units))
    def _():
        _cast_rows(x_ref, xb_ref, tile)

    @pl.when(n % pairs == 2)
    def _():
        _scale_rows(x_ref, y_ref, alpha, tile)

    @pl.when((n >= 1) & (j == 0) & (t == 0))
    def _():
        st_ref[...] = jnp.zeros_like(st_ref)
        cv_ref[...] = jnp.zeros_like(cv_ref)

    def stage(parity, do_project, do_elementwise, do_output):
        p_write, p_read = (p_even, p_odd) if parity == 0 else (p_odd, p_even)
        mix_write, mix_read = (mix_even, mix_odd) if parity == 0 else (mix_odd, mix_even)

        def output():
            y_ref[...] += jnp.dot(mix_read[...], wo_ref[...], preferred_element_type=F32)

        if not do_elementwise:
            if do_project:
                for i in range(COL_BLOCKS):
                    p_write[i] = jnp.dot(xb_ref[...], w_ref[i], preferred_element_type=F32)
            if do_output:
                output()
            return
        mix_ref = mix_write
        strip = _strip_reader(p_read)
        cw = cw_ref[...]
        na = na_ref[...]
        nb = nb_ref[...]
        lb = _lower_bound(lbl_ref[...])
        pos = lax.broadcasted_iota(jnp.int32, (CHUNK, hd), 0)
        causal = (lax.broadcasted_iota(jnp.int32, (CHUNK, CHUNK), 0)
                  >= lax.broadcasted_iota(jnp.int32, (CHUNK, CHUNK), 1))
        mid = CHUNK // 2 - 1

        def conv_chunk(rows, tail):
            u = strip(2, rows) * strip(0, rows)
            prev1 = jnp.broadcast_to(tail[SUBLANES - 1:SUBLANES], u.shape)
            prev2 = jnp.broadcast_to(tail[SUBLANES - 2:SUBLANES - 1], u.shape)
            u1 = jnp.where(pos == 0, prev1, pltpu.roll(u, 1, axis=0))
            u2 = jnp.where(pos == 0, prev2, jnp.where(pos == 1, prev1, pltpu.roll(u, 2, axis=0)))
            ya = _conv_apply(u, u1, u2, cw, strip(1, rows), strip(3, rows), na)
            mix_ref[rows, 0:hd] = ya.astype(BF16)
            return u[CHUNK - SUBLANES:CHUNK]

        def hgrn_chunk(rows):
            q, k, g = _gates(strip(4, rows), strip(5, rows), lb, hd)
            v = strip(6, rows)
            cum = _segment_cumsum(g, pos, CHUNK)
            c_mid = cum[mid:mid + 1]
            c_last = cum[CHUNK - 1:CHUNK]
            q_e = q * jnp.exp(cum - c_mid)
            k_e = k * jnp.exp(c_mid - cum)
            k_t = k * jnp.exp(c_last - cum)
            q_i = q_e * jnp.exp(c_mid)
            s_t = st_ref[j]
            scores = lax.dot_general(q_e.astype(BF16), k_e.astype(BF16), _NT,
                                     preferred_element_type=F32)
            probs = jnp.where(causal, scores, 0.0).astype(BF16)
            o = jnp.dot(probs, v.astype(BF16), preferred_element_type=F32)
            o = o + lax.dot_general(q_i.astype(BF16), s_t.astype(BF16), _NT,
                                    preferred_element_type=F32)
            st_ref[j] = jnp.exp(c_last) * s_t + jnp.dot(
                v.T.astype(BF16), k_t.astype(BF16), preferred_element_type=F32)
            yb = _lane_rms(o, nb) * _silu(strip(7, rows))
            mix_ref[rows, hd:2 * hd] = yb.astype(BF16)

        if do_output:
            output()
        tail = cv_ref[j]
        for i in range(COL_BLOCKS):
            if do_project:
                p_write[i] = jnp.dot(xb_ref[...], w_ref[i], preferred_element_type=F32)
            for c in range(chunks_per_block):
                rows = pl.ds((i * chunks_per_block + c) * CHUNK, CHUNK)
                tail = conv_chunk(rows, tail)
                hgrn_chunk(rows)
        cv_ref[j] = tail

        @pl.when(t == tiles_per_seq - 1)
        def _():
            cbuf_ref[b, :, pl.ds(pl.multiple_of(j * hd, hd), hd)] = (
                tail[SUBLANES - (CONV_WIDTH - 1):SUBLANES])
            snew_ref[b, j] = st_ref[j].T

    steady = (n >= 2) & (n < units)
    variants = [(n == 0, 0, (True, False, False)),
                (n == 1, 1, (True, True, False)),
                (steady & (n % 2 == 0), 0, (True, True, True)),
                (steady & (n % 2 == 1), 1, (True, True, True)),
                (n == units, units % 2, (False, True, True)),
                (n == units + 1, (units + 1) % 2, (False, False, True))]
    for cond, parity, flags in variants:
        pl.when(cond)(functools.partial(stage, parity, *flags))

    @pl.when((n >= 2) & ((n - 2) % pairs == pairs - 1))
    def _():
        _layer_norm_rows(y_ref, lg_ref[...], lbias_ref[...], tile)


def _sample_body(x_ref, hist_ref, s0_ref, w_ref, wo_ref, cw_ref, na_ref, lbl_ref, nb_ref,
                 lg_ref, lbias_ref,
                 y_ref, cbuf_ref, snew_ref,
                 xb_ref, p_even, p_odd, mix_ref, *, seqs, steps, alpha):
    jj = pl.program_id(0)
    c = pl.program_id(1)
    pairs = pl.num_programs(0) - 1
    hd = LANES
    rows_all = x_ref.shape[0]
    nrow = seqs * steps

    def project(p_write):
        p_write[c] = jnp.dot(xb_ref[...], w_ref[c], preferred_element_type=F32)

    @pl.when(jj == 0)
    def _():
        @pl.when(c == 0)
        def _():
            _init_rows(x_ref, xb_ref, y_ref, alpha, rows_all)
        project(p_even)

    def stage(p_write, p_read):
        project(p_write)
        rows = pl.ds(pl.multiple_of(c * nrow, nrow), nrow)
        strip = _strip_reader(p_read)
        pos = lax.broadcasted_iota(jnp.int32, (nrow, hd), 0) % steps
        per_row = lambda a: jnp.broadcast_to(a, (seqs, steps, hd)).reshape(nrow, hd)

        hist = hist_ref[...]
        prev2 = per_row(hist[:, 0:1, :])
        prev1 = per_row(hist[:, 1:2, :])
        u = strip(2, rows) * strip(0, rows)
        u1 = jnp.where(pos == 0, prev1, pltpu.roll(u, 1, axis=0))
        u2 = jnp.where(pos == 0, prev2, jnp.where(pos == 1, prev1, pltpu.roll(u, 2, axis=0)))
        ya = _conv_apply(u, u1, u2, cw_ref[...], strip(1, rows), strip(3, rows), na_ref[...])
        mix_ref[rows, 0:hd] = ya.astype(BF16)
        cbuf_ref[...] = u.reshape(seqs, steps, hd)[:, steps - (CONV_WIDTH - 1):, :]

        lb = _lower_bound(lbl_ref[...])
        q, k, g = _gates(strip(4, rows), strip(5, rows), lb, hd)
        v = strip(6, rows)
        cum = _segment_cumsum(g, pos, steps)
        c_last = per_row(cum.reshape(seqs, steps, hd)[:, steps - 1:steps, :])
        q_e = q * jnp.exp(cum)
        k_e = k * jnp.exp(-cum)
        k_t = k * jnp.exp(c_last - cum)
        v_b = v.astype(BF16)
        decay = jnp.exp(c_last)

        ri = lax.broadcasted_iota(jnp.int32, (nrow, nrow), 0)
        ci = lax.broadcasted_iota(jnp.int32, (nrow, nrow), 1)
        same_causal = (ri // steps == ci // steps) & (ri >= ci)
        scores = lax.dot_general(q_e.astype(BF16), k_e.astype(BF16), _NT,
                                 preferred_element_type=F32)
        probs = jnp.where(same_causal, scores, 0.0).astype(BF16)
        o_intra = jnp.dot(probs, v_b, preferred_element_type=F32)

        o_inter = []
        for s in range(seqs):
            sr = slice(s * steps, (s + 1) * steps)
            s0 = s0_ref[s]
            o_inter.append(jnp.dot(q_e[sr].astype(BF16), s0.astype(BF16),
                                   preferred_element_type=F32))
            upd = lax.dot_general(k_t[sr].astype(BF16), v[sr].astype(BF16), _TN,
                                  preferred_element_type=F32)
            decay_col = jnp.broadcast_to(decay[s * steps:s * steps + 1], (hd, hd)).T
            snew_ref[s] = decay_col * s0 + upd
        o = o_intra + jnp.concatenate(o_inter, axis=0)

        yb = _lane_rms(o, nb_ref[...]) * _silu(strip(7, rows))
        mix_ref[rows, hd:2 * hd] = yb.astype(BF16)

    @pl.when(jj > 0)
    def _():
        _by_parity(jj, stage, p_even, p_odd)

        @pl.when(c == pl.num_programs(1) - 1)
        def _():
            y_ref[...] += jnp.dot(mix_ref[...], wo_ref[...], preferred_element_type=F32)

            @pl.when(jj == pairs)
            def _():
                _layer_norm_rows(y_ref, lg_ref[...], lbias_ref[...], rows_all)


def kernel(x_prompt, x_sample, state_conv, state_hgrn, w_in, conv_w, norm_a, lb_logits,
           norm_b, w_out, ln_gain, ln_bias):
    batch, seq, d_model = x_prompt.shape
    dec_batch, dec_seq, _ = x_sample.shape
    depth, _, n_proj = w_in.shape
    assert depth == 1
    heads, dk, dv = state_hgrn.shape[2:]
    w_conv = state_conv.shape[-1]
    hd = LANES
    assert dk == hd and dv == hd and w_conv == heads * hd and n_proj == STRIPS * heads * hd
    assert dec_seq == SUBLANES and dec_batch % SAMPLE_SEQS == 0
    assert seq % PROMPT_TILE == 0 and PROMPT_TILE % (COL_BLOCKS * CHUNK) == 0
    assert dec_batch // SAMPLE_SEQS == COL_BLOCKS
    alpha = (2.0 * depth) ** 0.25
    f32 = x_prompt.dtype

    w_r, wo_r = _relayout_weights(w_in[0], w_out[0], heads)

    last = heads - 1
    w_spec = lambda im: pl.BlockSpec((None, COL_BLOCKS, d_model, MXU_COLS), im)
    wo_spec = lambda im: pl.BlockSpec((None, 2 * hd, d_model), im)

    def strip_spec(arr_rows, pair_of):
        return pl.BlockSpec((arr_rows, hd), lambda *g: (0, pair_of(*g)))

    tile = PROMPT_TILE
    tiles_per_seq = seq // tile
    units = batch * tiles_per_seq * heads
    unit_p = lambda n: jnp.minimum(n, units - 1)
    unit_e = lambda n: jnp.clip(n - 1, 0, units - 1)
    unit_o = lambda n: jnp.clip(n - 2, 0, units - 1)
    pair_p = lambda n: unit_p(n) % heads
    pair_e = lambda n: unit_e(n) % heads
    pair_o = lambda n: unit_o(n) % heads

    def row_tile_map(unit_of):
        def index_map(n):
            row_tile = unit_of(n) // heads
            return (row_tile // tiles_per_seq, row_tile % tiles_per_seq, 0)
        return index_map

    y_p, conv_p, hgrn_p = pl.pallas_call(
        functools.partial(_prompt_body, tile=tile, tiles_per_seq=tiles_per_seq, pairs=heads,
                          units=units, alpha=alpha),
        grid=(units + 2,),
        in_specs=[
            pl.BlockSpec((None, tile, d_model), row_tile_map(unit_p)),
            w_spec(lambda n: (pair_p(n), 0, 0, 0)),
            wo_spec(lambda n: (pair_o(n), 0, 0)),
            pl.BlockSpec((None, CONV_WIDTH, hd), lambda n: (0, 0, pair_e(n))),
            strip_spec(1, pair_e), strip_spec(lb_logits.shape[0], pair_e), strip_spec(1, pair_e),
            pl.BlockSpec((1, d_model), lambda n: (0, 0)),
            pl.BlockSpec((1, d_model), lambda n: (0, 0)),
        ],
        out_specs=[
            pl.BlockSpec((None, tile, d_model), row_tile_map(unit_o)),
            pl.BlockSpec((None, batch, CONV_WIDTH - 1, w_conv), lambda n: (0, 0, 0, 0)),
            pl.BlockSpec((None, batch, heads, hd, hd), lambda n: (0, 0, 0, 0, 0)),
        ],
        out_shape=[
            jax.ShapeDtypeStruct((batch, seq, d_model), f32),
            jax.ShapeDtypeStruct((depth, batch, CONV_WIDTH - 1, w_conv), state_conv.dtype),
            jax.ShapeDtypeStruct((depth, batch, heads, dk, dv), state_hgrn.dtype),
        ],
        scratch_shapes=[
            pltpu.VMEM((tile, d_model), BF16),
            pltpu.VMEM((COL_BLOCKS, tile, MXU_COLS), F32),
            pltpu.VMEM((COL_BLOCKS, tile, MXU_COLS), F32),
            pltpu.VMEM((tile, 2 * hd), BF16),
            pltpu.VMEM((tile, 2 * hd), BF16),
            pltpu.VMEM((heads, hd, hd), F32),
            pltpu.VMEM((heads, SUBLANES, hd), F32),
        ],
        compiler_params=pltpu.CompilerParams(
            dimension_semantics=("arbitrary",),
            vmem_limit_bytes=VMEM_LIMIT),
        name="prompt_mixer",
    )(x_prompt, w_r, wo_r, conv_w, norm_a, lb_logits, norm_b, ln_gain, ln_bias)

    rows_s = dec_batch * dec_seq
    xs = x_sample.reshape(rows_s, d_model)
    mm_s = lambda jj, c: jnp.minimum(jj, last)
    ew_s = lambda jj, c: jnp.maximum(jj - 1, 0)
    resident = lambda shape: pl.BlockSpec(shape, lambda jj, c: (0, 0),
                                          pipeline_mode=pl.Buffered(1))
    ch_s = lambda jj, c: jnp.where(jj == 0, 0, c)
    state_spec = pl.BlockSpec((None, SAMPLE_SEQS, None, hd, hd),
                              lambda jj, c: (0, ch_s(jj, c), ew_s(jj, c), 0, 0))
    hist_spec = pl.BlockSpec((None, SAMPLE_SEQS, CONV_WIDTH - 1, hd),
                             lambda jj, c: (0, ch_s(jj, c), 0, ew_s(jj, c)))
    y_s, conv_s, hgrn_s = pl.pallas_call(
        functools.partial(_sample_body, seqs=SAMPLE_SEQS, steps=dec_seq, alpha=alpha),
        grid=(heads + 1, dec_batch // SAMPLE_SEQS),
        in_specs=[
            resident((rows_s, d_model)),
            hist_spec,
            state_spec,
            w_spec(lambda jj, c: (mm_s(jj, c), 0, 0, 0)),
            wo_spec(lambda jj, c: (ew_s(jj, c), 0, 0)),
            pl.BlockSpec((None, CONV_WIDTH, hd), lambda jj, c: (0, 0, ew_s(jj, c))),
            strip_spec(1, ew_s), strip_spec(lb_logits.shape[0], ew_s), strip_spec(1, ew_s),
            pl.BlockSpec((1, d_model), lambda jj, c: (0, 0)),
            pl.BlockSpec((1, d_model), lambda jj, c: (0, 0)),
        ],
        out_specs=[resident((rows_s, d_model)), hist_spec, state_spec],
        out_shape=[
            jax.ShapeDtypeStruct((rows_s, d_model), f32),
            jax.ShapeDtypeStruct((depth, dec_batch, CONV_WIDTH - 1, w_conv), state_conv.dtype),
            jax.ShapeDtypeStruct((depth, dec_batch, heads, dk, dv), state_hgrn.dtype),
        ],
        scratch_shapes=[
            pltpu.VMEM((rows_s, d_model), BF16),
            pltpu.VMEM((COL_BLOCKS, rows_s, MXU_COLS), F32),
            pltpu.VMEM((COL_BLOCKS, rows_s, MXU_COLS), F32),
            pltpu.VMEM((rows_s, 2 * hd), BF16),
        ],
        compiler_params=pltpu.CompilerParams(
            dimension_semantics=("arbitrary", "arbitrary"),
            vmem_limit_bytes=VMEM_LIMIT),
        name="sample_mixer",
    )(xs, state_conv, state_hgrn, w_r, wo_r, conv_w, norm_a, lb_logits, norm_b, ln_gain, ln_bias)

    return (y_p, y_s.reshape(x_sample.shape), conv_p, hgrn_p, conv_s, hgrn_s)
```

```python
import functools

import jax
import jax.numpy as jnp
from jax import lax
from jax.experimental import pallas as pl
from jax.experimental.pallas import tpu as pltpu

LANES = 128
SUBLANES = 8
MXU_COLS = 256
STRIPS = 8
COL_BLOCKS = STRIPS * LANES // MXU_COLS
CONV_WIDTH = 3
EPS = 1e-5
PROMPT_TILE = 512
CHUNK = 128
SAMPLE_SEQS = 32
VMEM_LIMIT = 56 * 1024 * 1024

F32 = jnp.float32
BF16 = jnp.bfloat16
_NT = (((1,), (1,)), ((), ()))
_TN = (((0,), (0,)), ((), ()))


def _sigmoid(x):
    return 1.0 / (1.0 + jnp.exp(-x))


def _silu(x):
    return x * _sigmoid(x)


def _lane_rms(x, gain):
    return x * lax.rsqrt(jnp.mean(x * x, axis=-1, keepdims=True) + EPS) * gain


def _lower_bound(lbl):
    e = jnp.exp(lbl - jnp.max(lbl, axis=0, keepdims=True))
    return e[0:1] / jnp.sum(e, axis=0, keepdims=True)


def _segment_cumsum(g, pos, seg):
    s = 1
    while s < seg:
        g = g + jnp.where(pos >= s, pltpu.roll(g, s, axis=0), 0.0)
        s *= 2
    return g


def _layer_norm_rows(y_ref, gain, bias, rows):
    def body(i, _):
        r0 = pl.multiple_of(i * CHUNK, CHUNK)
        z = y_ref[pl.ds(r0, CHUNK), :]
        mu = jnp.mean(z, axis=-1, keepdims=True)
        zc = z - mu
        var = jnp.mean(zc * zc, axis=-1, keepdims=True)
        y_ref[pl.ds(r0, CHUNK), :] = zc * lax.rsqrt(var + EPS) * gain + bias
        return 0
    lax.fori_loop(0, rows // CHUNK, body, 0)


def _cast_rows(x_ref, xb_ref, rows):
    def body(i, _):
        r0 = pl.multiple_of(i * CHUNK, CHUNK)
        xb_ref[pl.ds(r0, CHUNK), :] = x_ref[pl.ds(r0, CHUNK), :].astype(BF16)
        return 0
    lax.fori_loop(0, rows // CHUNK, body, 0)


def _scale_rows(x_ref, y_ref, alpha, rows):
    def body(i, _):
        r0 = pl.multiple_of(i * CHUNK, CHUNK)
        y_ref[pl.ds(r0, CHUNK), :] = alpha * x_ref[pl.ds(r0, CHUNK), :]
        return 0
    lax.fori_loop(0, rows // CHUNK, body, 0)


def _gates(qb, fb, lb, dk):
    q = _silu(qb) * (dk ** -0.5)
    f = lb + (1.0 - lb) * _sigmoid(fb)
    return q, 1.0 - f, jnp.log(f)


def _strip_reader(p_ref):
    def strip(k, rows):
        lo = (k % 2) * LANES
        return p_ref[k // 2, rows, lo:lo + LANES]
    return strip


def _conv_apply(u, u1, u2, cw, b_a, z_a, na):
    conv = cw[0:1] * u2 + cw[1:2] * u1 + cw[2:3] * u
    return _lane_rms(b_a * conv, na) * _silu(z_a)


def _run_variants(variants, stage):
    for cond, parity, flags in variants:
        pl.when(cond)(functools.partial(stage, parity, *flags))


def _prompt_body(x_ref, w_ref, wo_ref, cw_ref, na_ref, lbl_ref, nb_ref, lg_ref, lbias_ref,
                 y_ref, cbuf_ref, snew_ref,
                 xb_ref, p_even, p_odd, mix_even, mix_odd, st_ref, cv_ref,
                 *, tile, tiles_per_seq, pairs, units, alpha):
    n = pl.program_id(0)
    hd = LANES
    chunks_per_block = tile // (COL_BLOCKS * CHUNK)
    unit_e = jnp.clip(n - 1, 0, units - 1)
    j = unit_e % pairs
    row_tile = unit_e // pairs
    b = row_tile // tiles_per_seq
    t = row_tile % tiles_per_seq

    @pl.when((n % pairs == 0) & (n < units))
    def _():
        _cast_rows(x_ref, xb_ref, tile)

    @pl.when(n % pairs == 2)
    def _():
        _scale_rows(x_ref, y_ref, alpha, tile)

    @pl.when((n >= 1) & (j == 0) & (t == 0))
    def _():
        st_ref[...] = jnp.zeros_like(st_ref)
        cv_ref[...] = jnp.zeros_like(cv_ref)

    def stage(parity, do_project, do_elementwise, do_output):
        p_write, p_read = (p_even, p_odd) if parity == 0 else (p_odd, p_even)
        mix_write, mix_read = (mix_even, mix_odd) if parity == 0 else (mix_odd, mix_even)

        def project(i):
            p_write[i] = jnp.dot(xb_ref[...], w_ref[i], preferred_element_type=F32)

        if do_output:
            y_ref[...] += jnp.dot(mix_read[...], wo_ref[...], preferred_element_type=F32)
        if not do_elementwise:
            if do_project:
                for i in range(COL_BLOCKS):
                    project(i)
            return
        strip = _strip_reader(p_read)
        cw = cw_ref[...]
        na = na_ref[...]
        nb = nb_ref[...]
        lb = _lower_bound(lbl_ref[...])
        pos = lax.broadcasted_iota(jnp.int32, (CHUNK, hd), 0)
        causal = (lax.broadcasted_iota(jnp.int32, (CHUNK, CHUNK), 0)
                  >= lax.broadcasted_iota(jnp.int32, (CHUNK, CHUNK), 1))
        mid = CHUNK // 2 - 1

        def conv_chunk(rows, tail):
            u = strip(2, rows) * strip(0, rows)
            prev1 = jnp.broadcast_to(tail[SUBLANES - 1:SUBLANES], u.shape)
            prev2 = jnp.broadcast_to(tail[SUBLANES - 2:SUBLANES - 1], u.shape)
            u1 = jnp.where(pos == 0, prev1, pltpu.roll(u, 1, axis=0))
            u2 = jnp.where(pos == 0, prev2, jnp.where(pos == 1, prev1, pltpu.roll(u, 2, axis=0)))
            ya = _conv_apply(u, u1, u2, cw, strip(1, rows), strip(3, rows), na)
            mix_write[rows, 0:hd] = ya.astype(BF16)
            return u[CHUNK - SUBLANES:CHUNK]

        def hgrn_chunk(rows):
            q, k, g = _gates(strip(4, rows), strip(5, rows), lb, hd)
            v = strip(6, rows)
            cum = _segment_cumsum(g, pos, CHUNK)
            c_mid = cum[mid:mid + 1]
            c_last = cum[CHUNK - 1:CHUNK]
            q_e = q * jnp.exp(cum - c_mid)
            k_e = k * jnp.exp(c_mid - cum)
            k_t = k * jnp.exp(c_last - cum)
            q_i = q_e * jnp.exp(c_mid)
            s_t = st_ref[j]
            scores = lax.dot_general(q_e.astype(BF16), k_e.astype(BF16), _NT,
                                     preferred_element_type=F32)
            probs = jnp.where(causal, scores, 0.0).astype(BF16)
            o = jnp.dot(probs, v.astype(BF16), preferred_element_type=F32)
            o = o + lax.dot_general(q_i.astype(BF16), s_t.astype(BF16), _NT,
                                    preferred_element_type=F32)
            st_ref[j] = jnp.exp(c_last) * s_t + jnp.dot(
                v.T.astype(BF16), k_t.astype(BF16), preferred_element_type=F32)
            yb = _lane_rms(o, nb) * _silu(strip(7, rows))
            mix_write[rows, hd:2 * hd] = yb.astype(BF16)

        tail = cv_ref[j]
        for i in range(COL_BLOCKS):
            if do_project:
                project(i)
            for c in range(chunks_per_block):
                rows = pl.ds((i * chunks_per_block + c) * CHUNK, CHUNK)
                tail = conv_chunk(rows, tail)
                hgrn_chunk(rows)
        cv_ref[j] = tail

        @pl.when(t == tiles_per_seq - 1)
        def _():
            cbuf_ref[b, :, pl.ds(pl.multiple_of(j * hd, hd), hd)] = (
                tail[SUBLANES - (CONV_WIDTH - 1):SUBLANES])
            snew_ref[b, j] = st_ref[j].T

    steady = (n >= 2) & (n < units)
    _run_variants([(n == 0, 0, (True, False, False)),
                   (n == 1, 1, (True, True, False)),
                   (steady & (n % 2 == 0), 0, (True, True, True)),
                   (steady & (n % 2 == 1), 1, (True, True, True)),
                   (n == units, units % 2, (False, True, True)),
                   (n == units + 1, (units + 1) % 2, (False, False, True))], stage)

    @pl.when((n >= 2) & ((n - 2) % pairs == pairs - 1))
    def _():
        _layer_norm_rows(y_ref, lg_ref[...], lbias_ref[...], tile)


def _sample_body(x_ref, hist_ref, s0_ref, wa_ref, wb_ref, woa_ref, wob_ref,
                 cw_ref, na_ref, lbl_ref, nb_ref, lg_ref, lbias_ref,
                 y_ref, cbuf_ref, snew_ref, wr_ref, wor_ref,
                 xb_ref, p_even, p_odd, mix_even, mix_odd, wo_b,
                 *, seqs, steps, pairs, alpha):
    jj = pl.program_id(0)
    c = pl.program_id(1)
    hd = LANES
    rows_all, d_model = x_ref.shape
    nrow = seqs * steps
    out_cols = d_model // COL_BLOCKS

    @pl.when((jj == 0) & (c == 0))
    def _():
        _cast_rows(x_ref, xb_ref, rows_all)
        _scale_rows(x_ref, y_ref, alpha, rows_all)

    @pl.when((jj >= 2) & (c == 0))
    def _():
        for ref, lo in ((woa_ref, 0), (wob_ref, hd)):
            wor_ref[lo:lo + hd, :] = ref[...].astype(BF16)
            for q in range(COL_BLOCKS):
                wo_b[q, lo:lo + hd, :] = ref[:, q * out_cols:(q + 1) * out_cols].astype(BF16)

    def stage(parity, do_project, do_elementwise, do_output):
        p_write, p_read = (p_even, p_odd) if parity == 0 else (p_odd, p_even)
        mix_write, mix_read = (mix_even, mix_odd) if parity == 0 else (mix_odd, mix_even)

        if do_project:
            wr_ref[:, 0:hd] = wa_ref[...].astype(BF16)
            wr_ref[:, hd:2 * hd] = wb_ref[...].astype(BF16)
            p_write[c] = jnp.dot(xb_ref[...], wr_ref[...], preferred_element_type=F32)
        if do_output:
            cols = pl.ds(pl.multiple_of(c * out_cols, out_cols), out_cols)
            y_ref[:, cols] += jnp.dot(mix_read[...], wo_b[c], preferred_element_type=F32)
        if not do_elementwise:
            return

        rows = pl.ds(pl.multiple_of(c * nrow, nrow), nrow)
        strip = _strip_reader(p_read)
        pos = lax.broadcasted_iota(jnp.int32, (nrow, hd), 0) % steps
        per_row = lambda a: jnp.broadcast_to(a, (seqs, steps, hd)).reshape(nrow, hd)

        hist = hist_ref[...]
        prev2 = per_row(hist[:, 0:1, :])
        prev1 = per_row(hist[:, 1:2, :])
        u = strip(2, rows) * strip(0, rows)
        u1 = jnp.where(pos == 0, prev1, pltpu.roll(u, 1, axis=0))
        u2 = jnp.where(pos == 0, prev2, jnp.where(pos == 1, prev1, pltpu.roll(u, 2, axis=0)))
        ya = _conv_apply(u, u1, u2, cw_ref[...], strip(1, rows), strip(3, rows), na_ref[...])
        mix_write[rows, 0:hd] = ya.astype(BF16)
        cbuf_ref[...] = u.reshape(seqs, steps, hd)[:, steps - (CONV_WIDTH - 1):, :]

        lb = _lower_bound(lbl_ref[...])
        q, k, g = _gates(strip(4, rows), strip(5, rows), lb, hd)
        v = strip(6, rows)
        cum = _segment_cumsum(g, pos, steps)
        c_last = per_row(cum.reshape(seqs, steps, hd)[:, steps - 1:steps, :])
        q_e = q * jnp.exp(cum)
        k_e = k * jnp.exp(-cum)
        k_t = k * jnp.exp(c_last - cum)
        decay = jnp.exp(c_last)

        ri = lax.broadcasted_iota(jnp.int32, (nrow, nrow), 0)
        ci = lax.broadcasted_iota(jnp.int32, (nrow, nrow), 1)
        same_causal = (ri // steps == ci // steps) & (ri >= ci)
        scores = lax.dot_general(q_e.astype(BF16), k_e.astype(BF16), _NT,
                                 preferred_element_type=F32)
        probs = jnp.where(same_causal, scores, 0.0).astype(BF16)
        o_intra = jnp.dot(probs, v.astype(BF16), preferred_element_type=F32)

        o_inter = []
        for s in range(seqs):
            sr = slice(s * steps, (s + 1) * steps)
            s0 = s0_ref[s]
            o_inter.append(jnp.dot(q_e[sr].astype(BF16), s0.astype(BF16),
                                   preferred_element_type=F32))
            upd = lax.dot_general(k_t[sr].astype(BF16), v[sr].astype(BF16), _TN,
                                  preferred_element_type=F32)
            decay_col = jnp.broadcast_to(decay[s * steps:s * steps + 1], (hd, hd)).T
            snew_ref[s] = decay_col * s0 + upd
        o = o_intra + jnp.concatenate(o_inter, axis=0)

        yb = _lane_rms(o, nb_ref[...]) * _silu(strip(7, rows))
        mix_write[rows, hd:2 * hd] = yb.astype(BF16)

    steady = (jj >= 2) & (jj < pairs)
    _run_variants([(jj == 0, 0, (True, False, False)),
                   (jj == 1, 1, (True, True, False)),
                   (steady & (jj % 2 == 0), 0, (True, True, True)),
                   (steady & (jj % 2 == 1), 1, (True, True, True)),
                   (jj == pairs, pairs % 2, (False, True, True)),
                   (jj == pairs + 1, (pairs + 1) % 2, (False, False, True))], stage)

    @pl.when((jj == pairs + 1) & (c == pl.num_programs(1) - 1))
    def _():
        _layer_norm_rows(y_ref, lg_ref[...], lbias_ref[...], rows_all)


def kernel(x_prompt, x_sample, state_conv, state_hgrn, w_in, conv_w, norm_a, lb_logits,
           norm_b, w_out, ln_gain, ln_bias):
    batch, seq, d_model = x_prompt.shape
    dec_batch, dec_seq, _ = x_sample.shape
    depth, _, n_proj = w_in.shape
    assert depth == 1
    heads, dk, dv = state_hgrn.shape[2:]
    w_conv = state_conv.shape[-1]
    hd = LANES
    assert dk == hd and dv == hd and w_conv == heads * hd and n_proj == STRIPS * heads * hd
    assert dec_seq == SUBLANES and dec_batch // SAMPLE_SEQS == COL_BLOCKS
    assert seq % PROMPT_TILE == 0 and PROMPT_TILE % (COL_BLOCKS * CHUNK) == 0
    assert d_model % (COL_BLOCKS * LANES) == 0
    alpha = (2.0 * depth) ** 0.25
    f32 = x_prompt.dtype
    last = heads - 1

    def strip_spec(arr_rows, pair_of):
        return pl.BlockSpec((arr_rows, hd), lambda *g: (0, pair_of(*g)))

    rows_s = dec_batch * dec_seq
    chunks = dec_batch // SAMPLE_SEQS
    xs = x_sample.reshape(rows_s, d_model)
    mm_s = lambda jj, c: jnp.minimum(jj, last)
    ew_s = lambda jj, c: jnp.clip(jj - 1, 0, last)
    out_s = lambda jj, c: jnp.clip(jj - 2, 0, last)
    mm_c = lambda jj, c: jnp.where(jj > last, chunks - 1, c)
    ew_c = lambda jj, c: jnp.where(jj == 0, 0, jnp.where(jj > heads, chunks - 1, c))
    resident = lambda shape: pl.BlockSpec(shape, lambda jj, c: (0, 0),
                                          pipeline_mode=pl.Buffered(1))
    state_spec = pl.BlockSpec((None, SAMPLE_SEQS, None, hd, hd),
                              lambda jj, c: (0, ew_c(jj, c), ew_s(jj, c), 0, 0))
    hist_spec = pl.BlockSpec((None, SAMPLE_SEQS, CONV_WIDTH - 1, hd),
                             lambda jj, c: (0, ew_c(jj, c), 0, ew_s(jj, c)))
    w_strip = lambda half: pl.BlockSpec(
        (d_model, hd), lambda jj, c: (0, (2 * mm_c(jj, c) + half) * heads + mm_s(jj, c)))
    wo_strip = lambda half: pl.BlockSpec(
        (hd, d_model), lambda jj, c: (half * heads + out_s(jj, c), 0))
    y_s, conv_s, hgrn_s, w_r, wo_r = pl.pallas_call(
        functools.partial(_sample_body, seqs=SAMPLE_SEQS, steps=dec_seq, pairs=heads,
                          alpha=alpha),
        grid=(heads + 2, chunks),
        in_specs=[
            resident((rows_s, d_model)),
            hist_spec,
            state_spec,
            w_strip(0), w_strip(1), wo_strip(0), wo_strip(1),
            pl.BlockSpec((None, CONV_WIDTH, hd), lambda jj, c: (0, 0, ew_s(jj, c))),
            strip_spec(1, ew_s), strip_spec(lb_logits.shape[0], ew_s), strip_spec(1, ew_s),
            pl.BlockSpec((1, d_model), lambda jj, c: (0, 0)),
            pl.BlockSpec((1, d_model), lambda jj, c: (0, 0)),
        ],
        out_specs=[
            resident((rows_s, d_model)),
            hist_spec,
            state_spec,
            pl.BlockSpec((None, None, d_model, MXU_COLS),
                         lambda jj, c: (mm_s(jj, c), mm_c(jj, c), 0, 0)),
            pl.BlockSpec((None, 2 * hd, d_model), lambda jj, c: (out_s(jj, c), 0, 0)),
        ],
        out_shape=[
            jax.ShapeDtypeStruct((rows_s, d_model), f32),
            jax.ShapeDtypeStruct((depth, dec_batch, CONV_WIDTH - 1, w_conv), state_conv.dtype),
            jax.ShapeDtypeStruct((depth, dec_batch, heads, dk, dv), state_hgrn.dtype),
            jax.ShapeDtypeStruct((heads, COL_BLOCKS, d_model, MXU_COLS), BF16),
            jax.ShapeDtypeStruct((heads, 2 * hd, d_model), BF16),
        ],
        scratch_shapes=[
            pltpu.VMEM((rows_s, d_model), BF16),
            pltpu.VMEM((COL_BLOCKS, rows_s, MXU_COLS), F32),
            pltpu.VMEM((COL_BLOCKS, rows_s, MXU_COLS), F32),
            pltpu.VMEM((rows_s, 2 * hd), BF16),
            pltpu.VMEM((rows_s, 2 * hd), BF16),
            pltpu.VMEM((COL_BLOCKS, 2 * hd, d_model // COL_BLOCKS), BF16),
        ],
        compiler_params=pltpu.CompilerParams(
            dimension_semantics=("arbitrary", "arbitrary"),
            vmem_limit_bytes=VMEM_LIMIT),
        name="sample_mixer",
    )(xs, state_conv, state_hgrn, w_in[0], w_in[0], w_out[0], w_out[0],
      conv_w, norm_a, lb_logits, norm_b, ln_gain, ln_bias)

    tile = PROMPT_TILE
    tiles_per_seq = seq // tile
    units = batch * tiles_per_seq * heads
    unit_p = lambda n: jnp.minimum(n, units - 1)
    unit_e = lambda n: jnp.clip(n - 1, 0, units - 1)
    unit_o = lambda n: jnp.clip(n - 2, 0, units - 1)
    pair_p = lambda n: unit_p(n) % heads
    pair_e = lambda n: unit_e(n) % heads
    pair_o = lambda n: unit_o(n) % heads

    def row_tile_map(unit_of):
        def index_map(n):
            row_tile = unit_of(n) // heads
            return (row_tile // tiles_per_seq, row_tile % tiles_per_seq, 0)
        return index_map

    y_p, conv_p, hgrn_p = pl.pallas_call(
        functools.partial(_prompt_body, tile=tile, tiles_per_seq=tiles_per_seq, pairs=heads,
                          units=units, alpha=alpha),
        grid=(units + 2,),
        in_specs=[
            pl.BlockSpec((None, tile, d_model), row_tile_map(unit_p)),
            pl.BlockSpec((None, COL_BLOCKS, d_model, MXU_COLS), lambda n: (pair_p(n), 0, 0, 0)),
            pl.BlockSpec((None, 2 * hd, d_model), lambda n: (pair_o(n), 0, 0)),
            pl.BlockSpec((None, CONV_WIDTH, hd), lambda n: (0, 0, pair_e(n))),
            strip_spec(1, pair_e), strip_spec(lb_logits.shape[0], pair_e), strip_spec(1, pair_e),
            pl.BlockSpec((1, d_model), lambda n: (0, 0)),
            pl.BlockSpec((1, d_model), lambda n: (0, 0)),
        ],
        out_specs=[
            pl.BlockSpec((None, tile, d_model), row_tile_map(unit_o)),
            pl.BlockSpec((None, batch, CONV_WIDTH - 1, w_conv), lambda n: (0, 0, 0, 0)),
            pl.BlockSpec((None, batch, heads, hd, hd), lambda n: (0, 0, 0, 0, 0)),
        ],
        out_shape=[
            jax.ShapeDtypeStruct((batch, seq, d_model), f32),
            jax.ShapeDtypeStruct((depth, batch, CONV_WIDTH - 1, w_conv), state_conv.dtype),
            jax.ShapeDtypeStruct((depth, batch, heads, dk, dv), state_hgrn.dtype),
        ],
        scratch_shapes=[
            pltpu.VMEM((tile, d_model), BF16),
            pltpu.VMEM((COL_BLOCKS, tile, MXU_COLS), F32),
            pltpu.VMEM((COL_BLOCKS, tile, MXU_COLS), F32),
            pltpu.VMEM((tile, 2 * hd), BF16),
            pltpu.VMEM((tile, 2 * hd), BF16),
            pltpu.VMEM((heads, hd, hd), F32),
            pltpu.VMEM((heads, SUBLANES, hd), F32),
        ],
        compiler_params=pltpu.CompilerParams(
            dimension_semantics=("arbitrary",),
            vmem_limit_bytes=VMEM_LIMIT),
        name="prompt_mixer",
    )(x_prompt, w_r, wo_r, conv_w, norm_a, lb_logits, norm_b, ln_gain, ln_bias)

    return (y_p, y_s.reshape(x_sample.shape), conv_p, hgrn_p, conv_s, hgrn_s)
```

```python
import functools

import jax
import jax.numpy as jnp
from jax import lax
from jax.experimental import pallas as pl
from jax.experimental.pallas import tpu as pltpu

LANES = 128
SUBLANES = 8
MXU_COLS = 256
STRIPS = 8
COL_BLOCKS = STRIPS * LANES // MXU_COLS
CONV_WIDTH = 3
EPS = 1e-5
PROMPT_TILE = 512
PROMPT_GROUP = 2
CHUNK = 128
SAMPLE_SEQS = 32
VMEM_LIMIT = 56 * 1024 * 1024

F32 = jnp.float32
BF16 = jnp.bfloat16
_NT = (((1,), (1,)), ((), ()))
_TN = (((0,), (0,)), ((), ()))


def _sigmoid(x):
    return 1.0 / (1.0 + jnp.exp(-x))


def _silu(x):
    return x * _sigmoid(x)


def _lane_rms(x, gain):
    return x * lax.rsqrt(jnp.mean(x * x, axis=-1, keepdims=True) + EPS) * gain


def _lower_bound(lbl):
    e = jnp.exp(lbl - jnp.max(lbl, axis=0, keepdims=True))
    return e[0:1] / jnp.sum(e, axis=0, keepdims=True)


def _segment_cumsum(g, pos, seg):
    s = 1
    while s < seg:
        g = g + jnp.where(pos >= s, pltpu.roll(g, s, axis=0), 0.0)
        s *= 2
    return g


def _layer_norm_rows(y_ref, gain, bias, rows):
    def body(i, _):
        r0 = pl.multiple_of(i * CHUNK, CHUNK)
        z = y_ref[pl.ds(r0, CHUNK), :]
        mu = jnp.mean(z, axis=-1, keepdims=True)
        zc = z - mu
        var = jnp.mean(zc * zc, axis=-1, keepdims=True)
        y_ref[pl.ds(r0, CHUNK), :] = zc * lax.rsqrt(var + EPS) * gain + bias
        return 0
    lax.fori_loop(0, rows // CHUNK, body, 0)


def _cast_rows(x_ref, xb_ref, rows):
    def body(i, _):
        r0 = pl.multiple_of(i * CHUNK, CHUNK)
        xb_ref[pl.ds(r0, CHUNK), :] = x_ref[pl.ds(r0, CHUNK), :].astype(BF16)
        return 0
    lax.fori_loop(0, rows // CHUNK, body, 0)


def _scale_rows(x_ref, y_ref, alpha, rows):
    def body(i, _):
        r0 = pl.multiple_of(i * CHUNK, CHUNK)
        y_ref[pl.ds(r0, CHUNK), :] = alpha * x_ref[pl.ds(r0, CHUNK), :]
        return 0
    lax.fori_loop(0, rows // CHUNK, body, 0)


def _gates(qb, fb, lb, dk):
    q = _silu(qb) * (dk ** -0.5)
    f = lb + (1.0 - lb) * _sigmoid(fb)
    return q, 1.0 - f, jnp.log(f)


def _strip_reader(p_ref):
    def strip(k, rows):
        lo = (k % 2) * LANES
        return p_ref[k // 2, rows, lo:lo + LANES]
    return strip


def _conv_apply(u, u1, u2, cw, b_a, z_a, na):
    conv = cw[0:1] * u2 + cw[1:2] * u1 + cw[2:3] * u
    return _lane_rms(b_a * conv, na) * _silu(z_a)


def _run_variants(variants, stage):
    for cond, parity, flags in variants:
        pl.when(cond)(functools.partial(stage, parity, *flags))


def _prompt_body(x_ref, w_ref, wo_ref, cw_ref, na_ref, lbl_ref, nb_ref, lg_ref, lbias_ref,
                 y_ref, cbuf_ref, snew_ref,
                 xb_ref, p_even, p_odd, mix_even, mix_odd, st_ref, cv_ref,
                 *, tile, tiles_per_seq, pairs, group, units, alpha):
    n = pl.program_id(0)
    hd = LANES
    chunks_per_block = tile // (COL_BLOCKS * CHUNK)
    per_tile = pairs // group
    unit_e = jnp.clip(n - 1, 0, units - 1)
    j0 = (unit_e % per_tile) * group
    row_tile = unit_e // per_tile
    t = row_tile % tiles_per_seq

    @pl.when((n % per_tile == 0) & (n < units))
    def _():
        _cast_rows(x_ref, xb_ref, tile)

    @pl.when(n % per_tile == 2)
    def _():
        _scale_rows(x_ref, y_ref, alpha, tile)

    @pl.when((n >= 1) & (j0 == 0) & (t == 0))
    def _():
        st_ref[...] = jnp.zeros_like(st_ref)
        cv_ref[...] = jnp.zeros_like(cv_ref)

    def stage(parity, do_project, do_elementwise, do_output):
        p_write, p_read = (p_even, p_odd) if parity == 0 else (p_odd, p_even)
        mix_write, mix_read = (mix_even, mix_odd) if parity == 0 else (mix_odd, mix_even)

        def project(g, i):
            p_write[g, i] = jnp.dot(xb_ref[...], w_ref[g, i], preferred_element_type=F32)

        def output():
            y_ref[...] += jnp.dot(mix_read[...], wo_ref[...], preferred_element_type=F32)

        if not do_elementwise:
            if do_project:
                for g in range(group):
                    for i in range(COL_BLOCKS):
                        project(g, i)
            if do_output:
                output()
            return
        pos = lax.broadcasted_iota(jnp.int32, (CHUNK, hd), 0)
        causal = (lax.broadcasted_iota(jnp.int32, (CHUNK, CHUNK), 0)
                  >= lax.broadcasted_iota(jnp.int32, (CHUNK, CHUNK), 1))
        mid = CHUNK // 2 - 1
        tails = [elementwise_unit(g, p_write, p_read, mix_write, do_project, pos, causal, mid)
                 for g in range(group)]
        if do_output:
            output()

        @pl.when(t == tiles_per_seq - 1)
        def _():
            for g in range(group):
                lanes = pl.ds(pl.multiple_of((j0 + g) * hd, hd), hd)
                cbuf_ref[:, lanes] = tails[g][SUBLANES - (CONV_WIDTH - 1):SUBLANES]
                snew_ref[j0 + g] = st_ref[j0 + g].T

    def elementwise_unit(g, p_write, p_read, mix_write, do_project, pos, causal, mid):
        j = j0 + g
        lanes = slice(g * hd, (g + 1) * hd)
        mix_lo = g * 2 * hd

        def strip(k, rows):
            lo = (k % 2) * hd
            return p_read[g, k // 2, rows, lo:lo + hd]

        cw = cw_ref[:, lanes]
        na = na_ref[:, lanes]
        nb = nb_ref[:, lanes]
        lb = _lower_bound(lbl_ref[:, lanes])

        def conv_chunk(rows, tail):
            u = strip(2, rows) * strip(0, rows)
            prev1 = jnp.broadcast_to(tail[SUBLANES - 1:SUBLANES], u.shape)
            prev2 = jnp.broadcast_to(tail[SUBLANES - 2:SUBLANES - 1], u.shape)
            u1 = jnp.where(pos == 0, prev1, pltpu.roll(u, 1, axis=0))
            u2 = jnp.where(pos == 0, prev2, jnp.where(pos == 1, prev1, pltpu.roll(u, 2, axis=0)))
            ya = _conv_apply(u, u1, u2, cw, strip(1, rows), strip(3, rows), na)
            mix_write[rows, mix_lo:mix_lo + hd] = ya.astype(BF16)
            return u[CHUNK - SUBLANES:CHUNK]

        def hgrn_chunk(rows):
            q, k, log_f = _gates(strip(4, rows), strip(5, rows), lb, hd)
            v = strip(6, rows)
            cum = _segment_cumsum(log_f, pos, CHUNK)
            c_mid = cum[mid:mid + 1]
            c_last = cum[CHUNK - 1:CHUNK]
            q_e = q * jnp.exp(cum - c_mid)
            k_e = k * jnp.exp(c_mid - cum)
            k_t = k * jnp.exp(c_last - cum)
            q_i = q_e * jnp.exp(c_mid)
            s_t = st_ref[j]
            scores = lax.dot_general(q_e.astype(BF16), k_e.astype(BF16), _NT,
                                     preferred_element_type=F32)
            probs = jnp.where(causal, scores, 0.0).astype(BF16)
            o = jnp.dot(probs, v.astype(BF16), preferred_element_type=F32)
            o = o + lax.dot_general(q_i.astype(BF16), s_t.astype(BF16), _NT,
                                    preferred_element_type=F32)
            st_ref[j] = jnp.exp(c_last) * s_t + jnp.dot(
                v.T.astype(BF16), k_t.astype(BF16), preferred_element_type=F32)
            yb = _lane_rms(o, nb) * _silu(strip(7, rows))
            mix_write[rows, mix_lo + hd:mix_lo + 2 * hd] = yb.astype(BF16)

        tail = cv_ref[j]
        for i in range(COL_BLOCKS):
            if do_project:
                p_write[g, i] = jnp.dot(xb_ref[...], w_ref[g, i], preferred_element_type=F32)
            for c in range(chunks_per_block):
                rows = pl.ds((i * chunks_per_block + c) * CHUNK, CHUNK)
                tail = conv_chunk(rows, tail)
                hgrn_chunk(rows)
        cv_ref[j] = tail
        return tail

    steady = (n >= 2) & (n < units)
    _run_variants([(n == 0, 0, (True, False, False)),
                   (n == 1, 1, (True, True, False)),
                   (steady & (n % 2 == 0), 0, (True, True, True)),
                   (steady & (n % 2 == 1), 1, (True, True, True)),
                   (n == units, units % 2, (False, True, True)),
                   (n == units + 1, (units + 1) % 2, (False, False, True))], stage)

    @pl.when((n >= 2) & ((n - 2) % per_tile == per_tile - 1))
    def _():
        _layer_norm_rows(y_ref, lg_ref[...], lbias_ref[...], tile)


def _sample_body(x_ref, hist_ref, s0_ref, wa_ref, wb_ref, woa_ref, wob_ref,
                 cw_ref, na_ref, lbl_ref, nb_ref, lg_ref, lbias_ref,
                 y_ref, cbuf_ref, snew_ref, wr_ref, wor_ref,
                 xb_ref, p_even, p_odd, mix_even, mix_odd, wo_b,
                 *, seqs, steps, pairs, alpha):
    jj = pl.program_id(0)
    c = pl.program_id(1)
    hd = LANES
    rows_all, d_model = x_ref.shape
    nrow = seqs * steps
    out_cols = d_model // COL_BLOCKS

    @pl.when((jj == 0) & (c == 0))
    def _():
        _cast_rows(x_ref, xb_ref, rows_all)
        _scale_rows(x_ref, y_ref, alpha, rows_all)

    @pl.when((jj >= 2) & (c == 0))
    def _():
        for ref, lo in ((woa_ref, 0), (wob_ref, hd)):
            wor_ref[lo:lo + hd, :] = ref[...].astype(BF16)
            for q in range(COL_BLOCKS):
                wo_b[q, lo:lo + hd, :] = ref[:, q * out_cols:(q + 1) * out_cols].astype(BF16)

    def stage(parity, do_project, do_elementwise, do_output):
        p_write, p_read = (p_even, p_odd) if parity == 0 else (p_odd, p_even)
        mix_write, mix_read = (mix_even, mix_odd) if parity == 0 else (mix_odd, mix_even)

        if do_project:
            wr_ref[:, 0:hd] = wa_ref[...].astype(BF16)
            wr_ref[:, hd:2 * hd] = wb_ref[...].astype(BF16)
            p_write[c] = jnp.dot(xb_ref[...], wr_ref[...], preferred_element_type=F32)
        if do_output:
            cols = pl.ds(pl.multiple_of(c * out_cols, out_cols), out_cols)
            y_ref[:, cols] += jnp.dot(mix_read[...], wo_b[c], preferred_element_type=F32)
        if not do_elementwise:
            return

        rows = pl.ds(pl.multiple_of(c * nrow, nrow), nrow)
        strip = _strip_reader(p_read)
        pos = lax.broadcasted_iota(jnp.int32, (nrow, hd), 0) % steps
        per_row = lambda a: jnp.broadcast_to(a, (seqs, steps, hd)).reshape(nrow, hd)

        hist = hist_ref[...]
        prev2 = per_row(hist[:, 0:1, :])
        prev1 = per_row(hist[:, 1:2, :])
        u = strip(2, rows) * strip(0, rows)
        u1 = jnp.where(pos == 0, prev1, pltpu.roll(u, 1, axis=0))
        u2 = jnp.where(pos == 0, prev2, jnp.where(pos == 1, prev1, pltpu.roll(u, 2, axis=0)))
        ya = _conv_apply(u, u1, u2, cw_ref[...], strip(1, rows), strip(3, rows), na_ref[...])
        mix_write[rows, 0:hd] = ya.astype(BF16)
        cbuf_ref[...] = u.reshape(seqs, steps, hd)[:, steps - (CONV_WIDTH - 1):, :]

        lb = _lower_bound(lbl_ref[...])
        q, k, g = _gates(strip(4, rows), strip(5, rows), lb, hd)
        v = strip(6, rows)
        cum = _segment_cumsum(g, pos, steps)
        c_last = per_row(cum.reshape(seqs, steps, hd)[:, steps - 1:steps, :])
        q_e = q * jnp.exp(cum)
        k_e = k * jnp.exp(-cum)
        k_t = k * jnp.exp(c_last - cum)
        decay = jnp.exp(c_last)

        ri = lax.broadcasted_iota(jnp.int32, (nrow, nrow), 0)
        ci = lax.broadcasted_iota(jnp.int32, (nrow, nrow), 1)
        same_causal = (ri // steps == ci // steps) & (ri >= ci)
        scores = lax.dot_general(q_e.astype(BF16), k_e.astype(BF16), _NT,
                                 preferred_element_type=F32)
        probs = jnp.where(same_causal, scores, 0.0).astype(BF16)
        o_intra = jnp.dot(probs, v.astype(BF16), preferred_element_type=F32)

        o_inter = []
        for s in range(seqs):
            sr = slice(s * steps, (s + 1) * steps)
            s0 = s0_ref[s]
            o_inter.append(jnp.dot(q_e[sr].astype(BF16), s0.astype(BF16),
                                   preferred_element_type=F32))
            upd = lax.dot_general(k_t[sr].astype(BF16), v[sr].astype(BF16), _TN,
                                  preferred_element_type=F32)
            decay_col = jnp.broadcast_to(decay[s * steps:s * steps + 1], (hd, hd)).T
            snew_ref[s] = decay_col * s0 + upd
        o = o_intra + jnp.concatenate(o_inter, axis=0)

        yb = _lane_rms(o, nb_ref[...]) * _silu(strip(7, rows))
        mix_write[rows, hd:2 * hd] = yb.astype(BF16)

    steady = (jj >= 2) & (jj < pairs)
    _run_variants([(jj == 0, 0, (True, False, False)),
                   (jj == 1, 1, (True, True, False)),
                   (steady & (jj % 2 == 0), 0, (True, True, True)),
                   (steady & (jj % 2 == 1), 1, (True, True, True)),
                   (jj == pairs, pairs % 2, (False, True, True)),
                   (jj == pairs + 1, (pairs + 1) % 2, (False, False, True))], stage)

    @pl.when((jj == pairs + 1) & (c == pl.num_programs(1) - 1))
    def _():
        _layer_norm_rows(y_ref, lg_ref[...], lbias_ref[...], rows_all)


def kernel(x_prompt, x_sample, state_conv, state_hgrn, w_in, conv_w, norm_a, lb_logits,
           norm_b, w_out, ln_gain, ln_bias):
    batch, seq, d_model = x_prompt.shape
    dec_batch, dec_seq, _ = x_sample.shape
    depth, _, n_proj = w_in.shape
    assert depth == 1
    heads, dk, dv = state_hgrn.shape[2:]
    w_conv = state_conv.shape[-1]
    hd = LANES
    assert dk == hd and dv == hd and w_conv == heads * hd and n_proj == STRIPS * heads * hd
    assert dec_seq == SUBLANES and dec_batch // SAMPLE_SEQS == COL_BLOCKS
    assert seq % PROMPT_TILE == 0 and PROMPT_TILE % (COL_BLOCKS * CHUNK) == 0
    assert heads % PROMPT_GROUP == 0 and heads // PROMPT_GROUP > 2
    assert d_model % (COL_BLOCKS * LANES) == 0
    alpha = (2.0 * depth) ** 0.25
    f32 = x_prompt.dtype
    last = heads - 1

    def strip_spec(arr_rows, pair_of):
        return pl.BlockSpec((arr_rows, hd), lambda *g: (0, pair_of(*g)))

    rows_s = dec_batch * dec_seq
    chunks = dec_batch // SAMPLE_SEQS
    xs = x_sample.reshape(rows_s, d_model)
    mm_s = lambda jj, c: jnp.minimum(jj, last)
    ew_s = lambda jj, c: jnp.clip(jj - 1, 0, last)
    out_s = lambda jj, c: jnp.clip(jj - 2, 0, last)
    mm_c = lambda jj, c: jnp.where(jj > last, chunks - 1, c)
    ew_c = lambda jj, c: jnp.where(jj == 0, 0, jnp.where(jj > heads, chunks - 1, c))
    resident = lambda shape: pl.BlockSpec(shape, lambda jj, c: (0, 0),
                                          pipeline_mode=pl.Buffered(1))
    state_spec = pl.BlockSpec((None, SAMPLE_SEQS, None, hd, hd),
                              lambda jj, c: (0, ew_c(jj, c), ew_s(jj, c), 0, 0))
    hist_spec = pl.BlockSpec((None, SAMPLE_SEQS, CONV_WIDTH - 1, hd),
                             lambda jj, c: (0, ew_c(jj, c), 0, ew_s(jj, c)))
    w_strip = lambda half: pl.BlockSpec(
        (d_model, hd), lambda jj, c: (0, (2 * mm_c(jj, c) + half) * heads + mm_s(jj, c)))
    wo_strip = lambda half: pl.BlockSpec(
        (hd, d_model), lambda jj, c: (half * heads + out_s(jj, c), 0))
    y_s, conv_s, hgrn_s, w_r, wo_r = pl.pallas_call(
        functools.partial(_sample_body, seqs=SAMPLE_SEQS, steps=dec_seq, pairs=heads,
                          alpha=alpha),
        grid=(heads + 2, chunks),
        in_specs=[
            resident((rows_s, d_model)),
            hist_spec,
            state_spec,
            w_strip(0), w_strip(1), wo_strip(0), wo_strip(1),
            pl.BlockSpec((None, CONV_WIDTH, hd), lambda jj, c: (0, 0, ew_s(jj, c))),
            strip_spec(1, ew_s), strip_spec(lb_logits.shape[0], ew_s), strip_spec(1, ew_s),
            pl.BlockSpec((1, d_model), lambda jj, c: (0, 0)),
            pl.BlockSpec((1, d_model), lambda jj, c: (0, 0)),
        ],
        out_specs=[
            resident((rows_s, d_model)),
            hist_spec,
            state_spec,
            pl.BlockSpec((None, None, d_model, MXU_COLS),
                         lambda jj, c: (mm_s(jj, c), mm_c(jj, c), 0, 0)),
            pl.BlockSpec((None, 2 * hd, d_model), lambda jj, c: (out_s(jj, c), 0, 0)),
        ],
        out_shape=[
            jax.ShapeDtypeStruct((rows_s, d_model), f32),
            jax.ShapeDtypeStruct((depth, dec_batch, CONV_WIDTH - 1, w_conv), state_conv.dtype),
            jax.ShapeDtypeStruct((depth, dec_batch, heads, dk, dv), state_hgrn.dtype),
            jax.ShapeDtypeStruct((heads, COL_BLOCKS, d_model, MXU_COLS), BF16),
            jax.ShapeDtypeStruct((heads, 2 * hd, d_model), BF16),
        ],
        scratch_shapes=[
            pltpu.VMEM((rows_s, d_model), BF16),
            pltpu.VMEM((COL_BLOCKS, rows_s, MXU_COLS), F32),
            pltpu.VMEM((COL_BLOCKS, rows_s, MXU_COLS), F32),
            pltpu.VMEM((rows_s, 2 * hd), BF16),
            pltpu.VMEM((rows_s, 2 * hd), BF16),
            pltpu.VMEM((COL_BLOCKS, 2 * hd, d_model // COL_BLOCKS), BF16),
        ],
        compiler_params=pltpu.CompilerParams(
            dimension_semantics=("arbitrary", "arbitrary"),
            vmem_limit_bytes=VMEM_LIMIT),
        name="sample_mixer",
    )(xs, state_conv, state_hgrn, w_in[0], w_in[0], w_out[0], w_out[0],
      conv_w, norm_a, lb_logits, norm_b, ln_gain, ln_bias)

    tile = PROMPT_TILE
    tiles_per_seq = seq // tile
    group = PROMPT_GROUP
    per_tile = heads // group
    units = batch * tiles_per_seq * per_tile
    unit_p = lambda n: jnp.minimum(n, units - 1)
    unit_e = lambda n: jnp.clip(n - 1, 0, units - 1)
    unit_o = lambda n: jnp.clip(n - 2, 0, units - 1)
    group_p = lambda n: unit_p(n) % per_tile
    group_e = lambda n: unit_e(n) % per_tile
    group_o = lambda n: unit_o(n) % per_tile
    batch_e = lambda n: unit_e(n) // (per_tile * tiles_per_seq)

    def row_tile_map(unit_of):
        def index_map(n):
            row_tile = unit_of(n) // per_tile
            return (row_tile // tiles_per_seq, row_tile % tiles_per_seq, 0)
        return index_map

    def group_strip(arr_rows):
        return pl.BlockSpec((arr_rows, group * hd), lambda n: (0, group_e(n)))

    y_p, conv_p, hgrn_p = pl.pallas_call(
        functools.partial(_prompt_body, tile=tile, tiles_per_seq=tiles_per_seq, pairs=heads,
                          group=group, units=units, alpha=alpha),
        grid=(units + 2,),
        in_specs=[
            pl.BlockSpec((None, tile, d_model), row_tile_map(unit_p)),
            pl.BlockSpec((group, COL_BLOCKS, d_model, MXU_COLS), lambda n: (group_p(n), 0, 0, 0)),
            pl.BlockSpec((group * 2 * hd, d_model), lambda n: (group_o(n), 0)),
            pl.BlockSpec((None, CONV_WIDTH, group * hd), lambda n: (0, 0, group_e(n))),
            group_strip(1), group_strip(lb_logits.shape[0]), group_strip(1),
            pl.BlockSpec((1, d_model), lambda n: (0, 0)),
            pl.BlockSpec((1, d_model), lambda n: (0, 0)),
        ],
        out_specs=[
            pl.BlockSpec((None, tile, d_model), row_tile_map(unit_o)),
            pl.BlockSpec((None, None, CONV_WIDTH - 1, w_conv), lambda n: (0, batch_e(n), 0, 0)),
            pl.BlockSpec((None, None, heads, hd, hd), lambda n: (0, batch_e(n), 0, 0, 0)),
        ],
        out_shape=[
            jax.ShapeDtypeStruct((batch, seq, d_model), f32),
            jax.ShapeDtypeStruct((depth, batch, CONV_WIDTH - 1, w_conv), state_conv.dtype),
            jax.ShapeDtypeStruct((depth, batch, heads, dk, dv), state_hgrn.dtype),
        ],
        scratch_shapes=[
            pltpu.VMEM((tile, d_model), BF16),
            pltpu.VMEM((group, COL_BLOCKS, tile, MXU_COLS), F32),
            pltpu.VMEM((group, COL_BLOCKS, tile, MXU_COLS), F32),
            pltpu.VMEM((tile, group * 2 * hd), BF16),
            pltpu.VMEM((tile, group * 2 * hd), BF16),
            pltpu.VMEM((heads, hd, hd), F32),
            pltpu.VMEM((heads, SUBLANES, hd), F32),
        ],
        compiler_params=pltpu.CompilerParams(
            dimension_semantics=("arbitrary",),
            vmem_limit_bytes=VMEM_LIMIT),
        name="prompt_mixer",
    )(x_prompt, w_r, wo_r.reshape(heads * 2 * hd, d_model), conv_w, norm_a, lb_logits, norm_b,
      ln_gain, ln_bias)

    return (y_p, y_s.reshape(x_sample.shape), conv_p, hgrn_p, conv_s, hgrn_s)
```

```python
import functools

import jax
import jax.numpy as jnp
from jax import lax
from jax.experimental import pallas as pl
from jax.experimental.pallas import tpu as pltpu

LANES = 128
SUBLANES = 8
MXU_COLS = 256
STRIPS = 8
COL_BLOCKS = STRIPS * LANES // MXU_COLS
PROMPT_COLS = 2 * MXU_COLS
PROMPT_COL_BLOCKS = STRIPS * LANES // PROMPT_COLS
CONV_WIDTH = 3
EPS = 1e-5
PROMPT_TILE = 512
PROMPT_GROUP = 2
CHUNK = 128
SAMPLE_SEQS = 32
VMEM_LIMIT = 56 * 1024 * 1024

F32 = jnp.float32
BF16 = jnp.bfloat16
_NT = (((1,), (1,)), ((), ()))
_TN = (((0,), (0,)), ((), ()))


def _sigmoid(x):
    return 1.0 / (1.0 + jnp.exp(-x))


def _silu(x):
    return x * _sigmoid(x)


def _lane_rms(x, gain):
    return x * lax.rsqrt(jnp.mean(x * x, axis=-1, keepdims=True) + EPS) * gain


def _lower_bound(lbl):
    e = jnp.exp(lbl - jnp.max(lbl, axis=0, keepdims=True))
    return e[0:1] / jnp.sum(e, axis=0, keepdims=True)


def _segment_cumsum(g, pos, seg):
    s = 1
    while s < min(seg, SUBLANES):
        g = g + jnp.where(pos >= s, pltpu.roll(g, s, axis=0), 0.0)
        s *= 2
    while s < seg:
        assert g.shape[0] == seg
        g = jnp.concatenate([g[:s], g[s:] + g[:-s]], axis=0)
        s *= 2
    return g


def _layer_norm_rows(y_ref, gain, bias, rows):
    def body(i, _):
        r0 = pl.multiple_of(i * CHUNK, CHUNK)
        z = y_ref[pl.ds(r0, CHUNK), :]
        mu = jnp.mean(z, axis=-1, keepdims=True)
        zc = z - mu
        var = jnp.mean(zc * zc, axis=-1, keepdims=True)
        y_ref[pl.ds(r0, CHUNK), :] = zc * lax.rsqrt(var + EPS) * gain + bias
        return 0
    lax.fori_loop(0, rows // CHUNK, body, 0)


def _cast_rows(x_ref, xb_ref, rows):
    def body(i, _):
        r0 = pl.multiple_of(i * CHUNK, CHUNK)
        xb_ref[pl.ds(r0, CHUNK), :] = x_ref[pl.ds(r0, CHUNK), :].astype(BF16)
        return 0
    lax.fori_loop(0, rows // CHUNK, body, 0)


def _scale_rows(x_ref, y_ref, alpha, rows):
    def body(i, _):
        r0 = pl.multiple_of(i * CHUNK, CHUNK)
        y_ref[pl.ds(r0, CHUNK), :] = alpha * x_ref[pl.ds(r0, CHUNK), :]
        return 0
    lax.fori_loop(0, rows // CHUNK, body, 0)


def _gates(qb, fb, lb, dk):
    q = _silu(qb) * (dk ** -0.5)
    f = lb + (1.0 - lb) * _sigmoid(fb)
    return q, 1.0 - f, jnp.log(f)


def _strip_reader(p_ref):
    def strip(k, rows):
        lo = (k % 2) * LANES
        return p_ref[k // 2, rows, lo:lo + LANES]
    return strip


def _conv_apply(u, u1, u2, cw, b_a, z_a, na):
    conv = cw[0:1] * u2 + cw[1:2] * u1 + cw[2:3] * u
    return _lane_rms(b_a * conv, na) * _silu(z_a)


def _run_variants(variants, stage):
    for cond, parity, flags in variants:
        pl.when(cond)(functools.partial(stage, parity, *flags))


def _prompt_body(x_ref, w_ref, wo_ref, cw_ref, na_ref, lbl_ref, nb_ref, lg_ref, lbias_ref,
                 y_ref, cbuf_ref, snew_ref,
                 xb_ref, p_even, p_odd, mix_even, mix_odd, st_ref, cv_ref,
                 *, tile, tiles_per_seq, pairs, group, units, alpha):
    n = pl.program_id(0)
    hd = LANES
    chunks_per_block = tile // (PROMPT_COL_BLOCKS * CHUNK)
    strips_per_block = PROMPT_COLS // hd
    per_tile = pairs // group
    unit_e = jnp.clip(n - 1, 0, units - 1)
    j0 = (unit_e % per_tile) * group
    row_tile = unit_e // per_tile
    t = row_tile % tiles_per_seq

    @pl.when((n % per_tile == 0) & (n < units))
    def _():
        _cast_rows(x_ref, xb_ref, tile)

    @pl.when(n % per_tile == 2)
    def _():
        _scale_rows(x_ref, y_ref, alpha, tile)

    @pl.when((n >= 1) & (j0 == 0) & (t == 0))
    def _():
        st_ref[...] = jnp.zeros_like(st_ref)
        cv_ref[...] = jnp.zeros_like(cv_ref)

    def stage(parity, do_project, do_elementwise, do_output):
        p_write, p_read = (p_even, p_odd) if parity == 0 else (p_odd, p_even)
        mix_write, mix_read = (mix_even, mix_odd) if parity == 0 else (mix_odd, mix_even)

        def project(g, i):
            p_write[g, i] = jnp.dot(xb_ref[...], w_ref[g, i], preferred_element_type=F32)

        def output():
            y_ref[...] += jnp.dot(mix_read[...], wo_ref[...], preferred_element_type=F32)

        if not do_elementwise:
            if do_project:
                for g in range(group):
                    for i in range(PROMPT_COL_BLOCKS):
                        project(g, i)
            if do_output:
                output()
            return
        pos = lax.broadcasted_iota(jnp.int32, (CHUNK, hd), 0)
        causal = (lax.broadcasted_iota(jnp.int32, (CHUNK, CHUNK), 0)
                  >= lax.broadcasted_iota(jnp.int32, (CHUNK, CHUNK), 1))
        mid = CHUNK // 2 - 1
        tails = [elementwise_unit(g, p_write, p_read, mix_write, do_project, pos, causal, mid)
                 for g in range(group)]
        if do_output:
            output()

        @pl.when(t == tiles_per_seq - 1)
        def _():
            for g in range(group):
                lanes = pl.ds(pl.multiple_of((j0 + g) * hd, hd), hd)
                cbuf_ref[:, lanes] = tails[g][SUBLANES - (CONV_WIDTH - 1):SUBLANES]
                snew_ref[j0 + g] = st_ref[j0 + g].T

    def elementwise_unit(g, p_write, p_read, mix_write, do_project, pos, causal, mid):
        j = j0 + g
        lanes = slice(g * hd, (g + 1) * hd)
        mix_lo = g * 2 * hd

        def strip(k, rows):
            lo = (k % strips_per_block) * hd
            return p_read[g, k // strips_per_block, rows, lo:lo + hd]

        cw = cw_ref[:, lanes]
        na = na_ref[:, lanes]
        nb = nb_ref[:, lanes]
        lb = _lower_bound(lbl_ref[:, lanes])

        def conv_chunk(rows, tail):
            u = strip(2, rows) * strip(0, rows)
            head = (SUBLANES, hd)
            prev1 = jnp.broadcast_to(tail[SUBLANES - 1:SUBLANES], head)
            prev2 = jnp.broadcast_to(tail[SUBLANES - 2:SUBLANES - 1], head)
            pos8 = pos[:SUBLANES]
            u1 = pltpu.roll(u, 1, axis=0)
            u1 = jnp.concatenate([jnp.where(pos8 == 0, prev1, u1[:SUBLANES]), u1[SUBLANES:]], axis=0)
            u2 = pltpu.roll(u, 2, axis=0)
            u2_head = jnp.where(pos8 == 0, prev2, jnp.where(pos8 == 1, prev1, u2[:SUBLANES]))
            u2 = jnp.concatenate([u2_head, u2[SUBLANES:]], axis=0)
            ya = _conv_apply(u, u1, u2, cw, strip(1, rows), strip(3, rows), na)
            mix_write[rows, mix_lo:mix_lo + hd] = ya.astype(BF16)
            return u[CHUNK - SUBLANES:CHUNK]

        def hgrn_chunk(rows):
            q, k, log_f = _gates(strip(4, rows), strip(5, rows), lb, hd)
            v = strip(6, rows)
            cum = _segment_cumsum(log_f, pos, CHUNK)
            c_mid = cum[mid:mid + 1]
            c_last = cum[CHUNK - 1:CHUNK]
            q_e = q * jnp.exp(cum - c_mid)
            k_e = k * jnp.exp(c_mid - cum)
            k_t = k_e * jnp.exp(c_last - c_mid)
            q_i = q_e * jnp.exp(c_mid)
            s_t = st_ref[j]
            scores = lax.dot_general(q_e.astype(BF16), k_e.astype(BF16), _NT,
                                     preferred_element_type=F32)
            probs = jnp.where(causal, scores, 0.0).astype(BF16)
            o = jnp.dot(probs, v.astype(BF16), preferred_element_type=F32)
            o = o + lax.dot_general(q_i.astype(BF16), s_t.astype(BF16), _NT,
                                    preferred_element_type=F32)
            st_ref[j] = jnp.exp(c_last) * s_t + jnp.dot(
                v.T.astype(BF16), k_t.astype(BF16), preferred_element_type=F32)
            yb = _lane_rms(o, nb) * _silu(strip(7, rows))
            mix_write[rows, mix_lo + hd:mix_lo + 2 * hd] = yb.astype(BF16)

        tail = cv_ref[j]
        for i in range(PROMPT_COL_BLOCKS):
            if do_project:
                p_write[g, i] = jnp.dot(xb_ref[...], w_ref[g, i], preferred_element_type=F32)
            for c in range(chunks_per_block):
                rows = pl.ds((i * chunks_per_block + c) * CHUNK, CHUNK)
                tail = conv_chunk(rows, tail)
                hgrn_chunk(rows)
        cv_ref[j] = tail
        return tail

    steady = (n >= 2) & (n < units)
    _run_variants([(n == 0, 0, (True, False, False)),
                   (n == 1, 1, (True, True, False)),
                   (steady & (n % 2 == 0), 0, (True, True, True)),
                   (steady & (n % 2 == 1), 1, (True, True, True)),
                   (n == units, units % 2, (False, True, True)),
                   (n == units + 1, (units + 1) % 2, (False, False, True))], stage)

    @pl.when((n >= 2) & ((n - 2) % per_tile == per_tile - 1))
    def _():
        _layer_norm_rows(y_ref, lg_ref[...], lbias_ref[...], tile)


def _sample_body(x_ref, hist_ref, s0_ref, wa_ref, wb_ref, woa_ref, wob_ref,
                 cw_ref, na_ref, lbl_ref, nb_ref, lg_ref, lbias_ref,
                 y_ref, cbuf_ref, snew_ref, wr_ref, wor_ref,
                 xb_ref, p_even, p_odd, mix_even, mix_odd, wo_b,
                 *, seqs, steps, pairs, alpha):
    jj = pl.program_id(0)
    c = pl.program_id(1)
    hd = LANES
    rows_all, d_model = x_ref.shape
    nrow = seqs * steps
    out_cols = d_model // COL_BLOCKS

    @pl.when((jj == 0) & (c == 0))
    def _():
        _cast_rows(x_ref, xb_ref, rows_all)
        _scale_rows(x_ref, y_ref, alpha, rows_all)

    @pl.when((jj >= 2) & (c == 0))
    def _():
        for ref, lo in ((woa_ref, 0), (wob_ref, hd)):
            wor_ref[lo:lo + hd, :] = ref[...].astype(BF16)
            for q in range(COL_BLOCKS):
                wo_b[q, lo:lo + hd, :] = ref[:, q * out_cols:(q + 1) * out_cols].astype(BF16)

    def stage(parity, do_project, do_elementwise, do_output):
        p_write, p_read = (p_even, p_odd) if parity == 0 else (p_odd, p_even)
        mix_write, mix_read = (mix_even, mix_odd) if parity == 0 else (mix_odd, mix_even)

        if do_project:
            wr_ref[:, 0:hd] = wa_ref[...].astype(BF16)
            wr_ref[:, hd:2 * hd] = wb_ref[...].astype(BF16)
            p_write[c] = jnp.dot(xb_ref[...], wr_ref[...], preferred_element_type=F32)
        if do_output:
            cols = pl.ds(pl.multiple_of(c * out_cols, out_cols), out_cols)
            y_ref[:, cols] += jnp.dot(mix_read[...], wo_b[c], preferred_element_type=F32)
        if not do_elementwise:
            return

        rows = pl.ds(pl.multiple_of(c * nrow, nrow), nrow)
        strip = _strip_reader(p_read)
        pos = lax.broadcasted_iota(jnp.int32, (nrow, hd), 0) % steps
        per_row = lambda a: jnp.broadcast_to(a, (seqs, steps, hd)).reshape(nrow, hd)

        hist = hist_ref[...]
        prev2 = per_row(hist[:, 0:1, :])
        prev1 = per_row(hist[:, 1:2, :])
        u = strip(2, rows) * strip(0, rows)
        u1 = jnp.where(pos == 0, prev1, pltpu.roll(u, 1, axis=0))
        u2 = jnp.where(pos == 0, prev2, jnp.where(pos == 1, prev1, pltpu.roll(u, 2, axis=0)))
        ya = _conv_apply(u, u1, u2, cw_ref[...], strip(1, rows), strip(3, rows), na_ref[...])
        mix_write[rows, 0:hd] = ya.astype(BF16)
        cbuf_ref[...] = u.reshape(seqs, steps, hd)[:, steps - (CONV_WIDTH - 1):, :]

        lb = _lower_bound(lbl_ref[...])
        q, k, g = _gates(strip(4, rows), strip(5, rows), lb, hd)
        v = strip(6, rows)
        cum = _segment_cumsum(g, pos, steps)
        c_last = per_row(cum.reshape(seqs, steps, hd)[:, steps - 1:steps, :])
        q_e = q * jnp.exp(cum)
        k_e = k * jnp.exp(-cum)
        k_t = k * jnp.exp(c_last - cum)
        decay = jnp.exp(c_last)

        ri = lax.broadcasted_iota(jnp.int32, (nrow, nrow), 0)
        ci = lax.broadcasted_iota(jnp.int32, (nrow, nrow), 1)
        same_causal = (ri // steps == ci // steps) & (ri >= ci)
        scores = lax.dot_general(q_e.astype(BF16), k_e.astype(BF16), _NT,
                                 preferred_element_type=F32)
        probs = jnp.where(same_causal, scores, 0.0).astype(BF16)
        o_intra = jnp.dot(probs, v.astype(BF16), preferred_element_type=F32)

        o_inter = []
        for s in range(seqs):
            sr = slice(s * steps, (s + 1) * steps)
            s0 = s0_ref[s]
            o_inter.append(jnp.dot(q_e[sr].astype(BF16), s0.astype(BF16),
                                   preferred_element_type=F32))
            upd = lax.dot_general(k_t[sr].astype(BF16), v[sr].astype(BF16), _TN,
                                  preferred_element_type=F32)
            decay_col = jnp.broadcast_to(decay[s * steps:s * steps + 1], (hd, hd)).T
            snew_ref[s] = decay_col * s0 + upd
        o = o_intra + jnp.concatenate(o_inter, axis=0)

        yb = _lane_rms(o, nb_ref[...]) * _silu(strip(7, rows))
        mix_write[rows, hd:2 * hd] = yb.astype(BF16)

    steady = (jj >= 2) & (jj < pairs)
    _run_variants([(jj == 0, 0, (True, False, False)),
                   (jj == 1, 1, (True, True, False)),
                   (steady & (jj % 2 == 0), 0, (True, True, True)),
                   (steady & (jj % 2 == 1), 1, (True, True, True)),
                   (jj == pairs, pairs % 2, (False, True, True)),
                   (jj == pairs + 1, (pairs + 1) % 2, (False, False, True))], stage)

    @pl.when((jj == pairs + 1) & (c == pl.num_programs(1) - 1))
    def _():
        _layer_norm_rows(y_ref, lg_ref[...], lbias_ref[...], rows_all)


def kernel(x_prompt, x_sample, state_conv, state_hgrn, w_in, conv_w, norm_a, lb_logits,
           norm_b, w_out, ln_gain, ln_bias):
    batch, seq, d_model = x_prompt.shape
    dec_batch, dec_seq, _ = x_sample.shape
    depth, _, n_proj = w_in.shape
    assert depth == 1
    heads, dk, dv = state_hgrn.shape[2:]
    w_conv = state_conv.shape[-1]
    hd = LANES
    assert dk == hd and dv == hd and w_conv == heads * hd and n_proj == STRIPS * heads * hd
    assert dec_seq == SUBLANES and dec_batch // SAMPLE_SEQS == COL_BLOCKS
    assert seq % PROMPT_TILE == 0 and PROMPT_TILE % (PROMPT_COL_BLOCKS * CHUNK) == 0
    wide = PROMPT_COLS // MXU_COLS
    assert heads % PROMPT_GROUP == 0 and heads // PROMPT_GROUP > 2
    assert d_model % (COL_BLOCKS * LANES) == 0
    alpha = (2.0 * depth) ** 0.25
    f32 = x_prompt.dtype
    last = heads - 1

    def strip_spec(arr_rows, pair_of):
        return pl.BlockSpec((arr_rows, hd), lambda *g: (0, pair_of(*g)))

    rows_s = dec_batch * dec_seq
    chunks = dec_batch // SAMPLE_SEQS
    xs = x_sample.reshape(rows_s, d_model)
    mm_s = lambda jj, c: jnp.minimum(jj, last)
    ew_s = lambda jj, c: jnp.clip(jj - 1, 0, last)
    out_s = lambda jj, c: jnp.clip(jj - 2, 0, last)
    mm_c = lambda jj, c: jnp.where(jj > last, chunks - 1, c)
    ew_c = lambda jj, c: jnp.where(jj == 0, 0, jnp.where(jj > heads, chunks - 1, c))
    resident = lambda shape: pl.BlockSpec(shape, lambda jj, c: (0, 0),
                                          pipeline_mode=pl.Buffered(1))
    state_spec = pl.BlockSpec((None, SAMPLE_SEQS, None, hd, hd),
                              lambda jj, c: (0, ew_c(jj, c), ew_s(jj, c), 0, 0))
    hist_spec = pl.BlockSpec((None, SAMPLE_SEQS, CONV_WIDTH - 1, hd),
                             lambda jj, c: (0, ew_c(jj, c), 0, ew_s(jj, c)))
    w_strip = lambda half: pl.BlockSpec(
        (d_model, hd), lambda jj, c: (0, (2 * mm_c(jj, c) + half) * heads + mm_s(jj, c)))
    wo_strip = lambda half: pl.BlockSpec(
        (hd, d_model), lambda jj, c: (half * heads + out_s(jj, c), 0))
    y_s, conv_s, hgrn_s, w_r, wo_r = pl.pallas_call(
        functools.partial(_sample_body, seqs=SAMPLE_SEQS, steps=dec_seq, pairs=heads,
                          alpha=alpha),
        grid=(heads + 2, chunks),
        in_specs=[
            resident((rows_s, d_model)),
            hist_spec,
            state_spec,
            w_strip(0), w_strip(1), wo_strip(0), wo_strip(1),
            pl.BlockSpec((None, CONV_WIDTH, hd), lambda jj, c: (0, 0, ew_s(jj, c))),
            strip_spec(1, ew_s), strip_spec(lb_logits.shape[0], ew_s), strip_spec(1, ew_s),
            pl.BlockSpec((1, d_model), lambda jj, c: (0, 0)),
            pl.BlockSpec((1, d_model), lambda jj, c: (0, 0)),
        ],
        out_specs=[
            resident((rows_s, d_model)),
            hist_spec,
            state_spec,
            pl.BlockSpec((None, None, d_model, MXU_COLS),
                         lambda jj, c: (mm_s(jj, c), mm_c(jj, c) // wide, 0, mm_c(jj, c) % wide)),
            pl.BlockSpec((None, 2 * hd, d_model), lambda jj, c: (out_s(jj, c), 0, 0)),
        ],
        out_shape=[
            jax.ShapeDtypeStruct((rows_s, d_model), f32),
            jax.ShapeDtypeStruct((depth, dec_batch, CONV_WIDTH - 1, w_conv), state_conv.dtype),
            jax.ShapeDtypeStruct((depth, dec_batch, heads, dk, dv), state_hgrn.dtype),
            jax.ShapeDtypeStruct((heads, PROMPT_COL_BLOCKS, d_model, PROMPT_COLS), BF16),
            jax.ShapeDtypeStruct((heads, 2 * hd, d_model), BF16),
        ],
        scratch_shapes=[
            pltpu.VMEM((rows_s, d_model), BF16),
            pltpu.VMEM((COL_BLOCKS, rows_s, MXU_COLS), F32),
            pltpu.VMEM((COL_BLOCKS, rows_s, MXU_COLS), F32),
            pltpu.VMEM((rows_s, 2 * hd), BF16),
            pltpu.VMEM((rows_s, 2 * hd), BF16),
            pltpu.VMEM((COL_BLOCKS, 2 * hd, d_model // COL_BLOCKS), BF16),
        ],
        compiler_params=pltpu.CompilerParams(
            dimension_semantics=("arbitrary", "arbitrary"),
            vmem_limit_bytes=VMEM_LIMIT),
        name="sample_mixer",
    )(xs, state_conv, state_hgrn, w_in[0], w_in[0], w_out[0], w_out[0],
      conv_w, norm_a, lb_logits, norm_b, ln_gain, ln_bias)

    tile = PROMPT_TILE
    tiles_per_seq = seq // tile
    group = PROMPT_GROUP
    per_tile = heads // group
    units = batch * tiles_per_seq * per_tile
    unit_p = lambda n: jnp.minimum(n, units - 1)
    unit_e = lambda n: jnp.clip(n - 1, 0, units - 1)
    unit_o = lambda n: jnp.clip(n - 2, 0, units - 1)
    group_p = lambda n: unit_p(n) % per_tile
    group_e = lambda n: unit_e(n) % per_tile
    group_o = lambda n: unit_o(n) % per_tile
    batch_e = lambda n: unit_e(n) // (per_tile * tiles_per_seq)

    def row_tile_map(unit_of):
        def index_map(n):
            row_tile = unit_of(n) // per_tile
            return (row_tile // tiles_per_seq, row_tile % tiles_per_seq, 0)
        return index_map

    def group_strip(arr_rows):
        return pl.BlockSpec((arr_rows, group * hd), lambda n: (0, group_e(n)))

    y_p, conv_p, hgrn_p = pl.pallas_call(
        functools.partial(_prompt_body, tile=tile, tiles_per_seq=tiles_per_seq, pairs=heads,
                          group=group, units=units, alpha=alpha),
        grid=(units + 2,),
        in_specs=[
            pl.BlockSpec((None, tile, d_model), row_tile_map(unit_p)),
            pl.BlockSpec((group, PROMPT_COL_BLOCKS, d_model, PROMPT_COLS),
                         lambda n: (group_p(n), 0, 0, 0)),
            pl.BlockSpec((group * 2 * hd, d_model), lambda n: (group_o(n), 0)),
            pl.BlockSpec((None, CONV_WIDTH, group * hd), lambda n: (0, 0, group_e(n))),
            group_strip(1), group_strip(lb_logits.shape[0]), group_strip(1),
            pl.BlockSpec((1, d_model), lambda n: (0, 0)),
            pl.BlockSpec((1, d_model), lambda n: (0, 0)),
        ],
        out_specs=[
            pl.BlockSpec((None, tile, d_model), row_tile_map(unit_o)),
            pl.BlockSpec((None, None, CONV_WIDTH - 1, w_conv), lambda n: (0, batch_e(n), 0, 0)),
            pl.BlockSpec((None, None, heads, hd, hd), lambda n: (0, batch_e(n), 0, 0, 0)),
        ],
        out_shape=[
            jax.ShapeDtypeStruct((batch, seq, d_model), f32),
            jax.ShapeDtypeStruct((depth, batch, CONV_WIDTH - 1, w_conv), state_conv.dtype),
            jax.ShapeDtypeStruct((depth, batch, heads, dk, dv), state_hgrn.dtype),
        ],
        scratch_shapes=[
            pltpu.VMEM((tile, d_model), BF16),
            pltpu.VMEM((group, PROMPT_COL_BLOCKS, tile, PROMPT_COLS), F32),
            pltpu.VMEM((group, PROMPT_COL_BLOCKS, tile, PROMPT_COLS), F32),
            pltpu.VMEM((tile, group * 2 * hd), BF16),
            pltpu.VMEM((tile, group * 2 * hd), BF16),
            pltpu.VMEM((heads, hd, hd), F32),
            pltpu.VMEM((heads, SUBLANES, hd), F32),
        ],
        compiler_params=pltpu.CompilerParams(
            dimension_semantics=("arbitrary",),
            vmem_limit_bytes=VMEM_LIMIT),
        name="prompt_mixer",
    )(x_prompt, w_r, wo_r.reshape(heads * 2 * hd, d_model), conv_w, norm_a, lb_logits, norm_b,
      ln_gain, ln_bias)

    return (y_p, y_s.reshape(x_sample.shape), conv_p, hgrn_p, conv_s, hgrn_s)
```

```python
import functools

import jax
import jax.numpy as jnp
from jax import lax
from jax.experimental import pallas as pl
from jax.experimental.pallas import tpu as pltpu

LANES = 128
SUBLANES = 8
MXU_COLS = 256
STRIPS = 8
COL_BLOCKS = STRIPS * LANES // MXU_COLS
PROMPT_COLS = 2 * MXU_COLS
PROMPT_COL_BLOCKS = STRIPS * LANES // PROMPT_COLS
CONV_WIDTH = 3
EPS = 1e-5
PROMPT_TILE = 512
PROMPT_GROUP = 2
CHUNK = 128
SAMPLE_SEQS = 32
VMEM_LIMIT = 56 * 1024 * 1024

F32 = jnp.float32
BF16 = jnp.bfloat16
_NT = (((1,), (1,)), ((), ()))
_TN = (((0,), (0,)), ((), ()))


def _sigmoid(x):
    return 1.0 / (1.0 + jnp.exp(-x))


def _silu(x):
    return x * _sigmoid(x)


def _lane_rms(x, gain):
    return x * lax.rsqrt(jnp.mean(x * x, axis=-1, keepdims=True) + EPS) * gain


def _lower_bound(lbl):
    e = jnp.exp(lbl - jnp.max(lbl, axis=0, keepdims=True))
    return e[0:1] / jnp.sum(e, axis=0, keepdims=True)


def _segment_cumsum(g, pos, seg):
    s = 1
    while s < min(seg, SUBLANES):
        g = g + jnp.where(pos >= s, pltpu.roll(g, s, axis=0), 0.0)
        s *= 2
    while s < seg:
        assert g.shape[0] == seg
        g = jnp.concatenate([g[:s], g[s:] + g[:-s]], axis=0)
        s *= 2
    return g


def _layer_norm_rows(y_ref, gain, bias, rows):
    def body(i, _):
        r0 = pl.multiple_of(i * CHUNK, CHUNK)
        z = y_ref[pl.ds(r0, CHUNK), :]
        mu = jnp.mean(z, axis=-1, keepdims=True)
        zc = z - mu
        var = jnp.mean(zc * zc, axis=-1, keepdims=True)
        y_ref[pl.ds(r0, CHUNK), :] = zc * lax.rsqrt(var + EPS) * gain + bias
        return 0
    lax.fori_loop(0, rows // CHUNK, body, 0)


def _cast_rows(x_ref, xb_ref, rows):
    def body(i, _):
        r0 = pl.multiple_of(i * CHUNK, CHUNK)
        xb_ref[pl.ds(r0, CHUNK), :] = x_ref[pl.ds(r0, CHUNK), :].astype(BF16)
        return 0
    lax.fori_loop(0, rows // CHUNK, body, 0)


def _scale_rows(x_ref, y_ref, alpha, rows):
    def body(i, _):
        r0 = pl.multiple_of(i * CHUNK, CHUNK)
        y_ref[pl.ds(r0, CHUNK), :] = alpha * x_ref[pl.ds(r0, CHUNK), :]
        return 0
    lax.fori_loop(0, rows // CHUNK, body, 0)


def _gates(qb, fb, lb, dk):
    q = _silu(qb) * (dk ** -0.5)
    f = lb + (1.0 - lb) * _sigmoid(fb)
    return q, 1.0 - f, jnp.log(f)


def _strip_reader(p_ref):
    def strip(k, rows):
        lo = (k % 2) * LANES
        return p_ref[k // 2, rows, lo:lo + LANES]
    return strip


def _conv_apply(u, u1, u2, cw, b_a, z_a, na):
    conv = cw[0:1] * u2 + cw[1:2] * u1 + cw[2:3] * u
    return _lane_rms(b_a * conv, na) * _silu(z_a)


def _run_variants(variants, stage):
    for cond, parity, flags in variants:
        pl.when(cond)(functools.partial(stage, parity, *flags))


def _prompt_body(x_ref, w_ref, wo_ref, cw_ref, na_ref, lbl_ref, nb_ref, lg_ref, lbias_ref,
                 y_ref, cbuf_ref, snew_ref,
                 xb_ref, p_even, p_odd, mix_even, mix_odd, st_ref, cv_ref,
                 *, tile, tiles_per_seq, pairs, group, units, alpha):
    n = pl.program_id(0)
    hd = LANES
    chunks_per_block = tile // (PROMPT_COL_BLOCKS * CHUNK)
    strips_per_block = PROMPT_COLS // hd
    per_tile = pairs // group
    unit_e = jnp.clip(n - 1, 0, units - 1)
    j0 = (unit_e % per_tile) * group
    row_tile = unit_e // per_tile
    t = row_tile % tiles_per_seq

    @pl.when((n % per_tile == 0) & (n < units))
    def _():
        _cast_rows(x_ref, xb_ref, tile)

    @pl.when(n % per_tile == 2)
    def _():
        _scale_rows(x_ref, y_ref, alpha, tile)

    @pl.when((n >= 1) & (j0 == 0) & (t == 0))
    def _():
        st_ref[...] = jnp.zeros_like(st_ref)
        cv_ref[...] = jnp.zeros_like(cv_ref)

    def stage(parity, do_project, do_elementwise, do_output):
        p_write, p_read = (p_even, p_odd) if parity == 0 else (p_odd, p_even)
        mix_write, mix_read = (mix_even, mix_odd) if parity == 0 else (mix_odd, mix_even)

        def project(g, i):
            p_write[g, i] = jnp.dot(xb_ref[...], w_ref[g, i], preferred_element_type=F32)

        def output():
            y_ref[...] += jnp.dot(mix_read[...], wo_ref[...], preferred_element_type=F32)

        if not do_elementwise:
            if do_project:
                for g in range(group):
                    for i in range(PROMPT_COL_BLOCKS):
                        project(g, i)
            if do_output:
                output()
            return
        pos = lax.broadcasted_iota(jnp.int32, (CHUNK, hd), 0)
        causal = (lax.broadcasted_iota(jnp.int32, (CHUNK, CHUNK), 0)
                  >= lax.broadcasted_iota(jnp.int32, (CHUNK, CHUNK), 1))
        mid = CHUNK // 2 - 1
        tails = [elementwise_unit(g, p_write, p_read, mix_write, do_project, pos, causal, mid)
                 for g in range(group)]
        if do_output:
            output()

        @pl.when(t == tiles_per_seq - 1)
        def _():
            for g in range(group):
                lanes = pl.ds(pl.multiple_of((j0 + g) * hd, hd), hd)
                cbuf_ref[:, lanes] = tails[g][SUBLANES - (CONV_WIDTH - 1):SUBLANES]
                snew_ref[j0 + g] = st_ref[j0 + g].T

    def elementwise_unit(g, p_write, p_read, mix_write, do_project, pos, causal, mid):
        j = j0 + g
        lanes = slice(g * hd, (g + 1) * hd)
        mix_lo = g * 2 * hd

        def strip(k, rows):
            lo = (k % strips_per_block) * hd
            return p_read[g, k // strips_per_block, rows, lo:lo + hd]

        cw = cw_ref[:, lanes]
        na = na_ref[:, lanes]
        nb = nb_ref[:, lanes]
        lb = _lower_bound(lbl_ref[:, lanes])

        def conv_chunk(rows, tail):
            u = strip(2, rows) * strip(0, rows)
            head = (SUBLANES, hd)
            prev1 = jnp.broadcast_to(tail[SUBLANES - 1:SUBLANES], head)
            prev2 = jnp.broadcast_to(tail[SUBLANES - 2:SUBLANES - 1], head)
            pos8 = pos[:SUBLANES]
            u1 = pltpu.roll(u, 1, axis=0)
            u1 = jnp.concatenate([jnp.where(pos8 == 0, prev1, u1[:SUBLANES]), u1[SUBLANES:]], axis=0)
            u2 = pltpu.roll(u, 2, axis=0)
            u2_head = jnp.where(pos8 == 0, prev2, jnp.where(pos8 == 1, prev1, u2[:SUBLANES]))
            u2 = jnp.concatenate([u2_head, u2[SUBLANES:]], axis=0)
            ya = _conv_apply(u, u1, u2, cw, strip(1, rows), strip(3, rows), na)
            mix_write[rows, mix_lo:mix_lo + hd] = ya.astype(BF16)
            return u[CHUNK - SUBLANES:CHUNK]

        def hgrn_chunk(rows):
            q, k, log_f = _gates(strip(4, rows), strip(5, rows), lb, hd)
            v = strip(6, rows)
            cum = _segment_cumsum(log_f, pos, CHUNK)
            c_mid = cum[mid:mid + 1]
            c_last = cum[CHUNK - 1:CHUNK]
            q_e = q * jnp.exp(cum - c_mid)
            k_e = k * jnp.exp(c_mid - cum)
            k_t = k_e * jnp.exp(c_last - c_mid)
            q_i = q_e * jnp.exp(c_mid)
            s_t = st_ref[j]
            scores = lax.dot_general(q_e.astype(BF16), k_e.astype(BF16), _NT,
                                     preferred_element_type=F32)
            probs = jnp.where(causal, scores, 0.0).astype(BF16)
            o = jnp.dot(probs, v.astype(BF16), preferred_element_type=F32)
            o = o + lax.dot_general(q_i.astype(BF16), s_t.astype(BF16), _NT,
                                    preferred_element_type=F32)
            st_ref[j] = jnp.exp(c_last) * s_t + jnp.dot(
                v.T.astype(BF16), k_t.astype(BF16), preferred_element_type=F32)
            yb = _lane_rms(o, nb) * _silu(strip(7, rows))
            mix_write[rows, mix_lo + hd:mix_lo + 2 * hd] = yb.astype(BF16)

        tail = cv_ref[j]
        for i in range(PROMPT_COL_BLOCKS):
            if do_project:
                p_write[g, i] = jnp.dot(xb_ref[...], w_ref[g, i], preferred_element_type=F32)
            for c in range(chunks_per_block):
                rows = pl.ds((i * chunks_per_block + c) * CHUNK, CHUNK)
                tail = conv_chunk(rows, tail)
                hgrn_chunk(rows)
        cv_ref[j] = tail
        return tail

    steady = (n >= 2) & (n < units)
    _run_variants([(n == 0, 0, (True, False, False)),
                   (n == 1, 1, (True, True, False)),
                   (steady & (n % 2 == 0), 0, (True, True, True)),
                   (steady & (n % 2 == 1), 1, (True, True, True)),
                   (n == units, units % 2, (False, True, True)),
                   (n == units + 1, (units + 1) % 2, (False, False, True))], stage)

    @pl.when((n >= 2) & ((n - 2) % per_tile == per_tile - 1))
    def _():
        _layer_norm_rows(y_ref, lg_ref[...], lbias_ref[...], tile)


def _sample_body(x_ref, hist_ref, s0_ref, wa_ref, wb_ref, woa_ref, wob_ref,
                 cw_ref, na_ref, lbl_ref, nb_ref, lg_ref, lbias_ref,
                 y_ref, cbuf_ref, snew_ref, wr_ref, wor_ref,
                 xb_ref, p_even, p_odd, mix_even, mix_odd, wo_b,
                 *, seqs, steps, pairs, alpha):
    jj = pl.program_id(0)
    c = pl.program_id(1)
    hd = LANES
    rows_all, d_model = x_ref.shape
    nrow = seqs * steps
    out_cols = d_model // COL_BLOCKS

    @pl.when((jj == 0) & (c == 0))
    def _():
        _cast_rows(x_ref, xb_ref, rows_all)
        _scale_rows(x_ref, y_ref, alpha, rows_all)

    @pl.when((jj >= 2) & (c == 0))
    def _():
        for ref, lo in ((woa_ref, 0), (wob_ref, hd)):
            wor_ref[lo:lo + hd, :] = ref[...].astype(BF16)
            for q in range(COL_BLOCKS):
                wo_b[q, lo:lo + hd, :] = ref[:, q * out_cols:(q + 1) * out_cols].astype(BF16)

    def stage(cs, parity, do_project, do_elementwise, do_output):
        p_write, p_read = (p_even, p_odd) if parity == 0 else (p_odd, p_even)
        mix_write, mix_read = (mix_even, mix_odd) if parity == 0 else (mix_odd, mix_even)

        if do_project:
            wr_ref[:, 0:hd] = wa_ref[...].astype(BF16)
            wr_ref[:, hd:2 * hd] = wb_ref[...].astype(BF16)
            p_write[cs] = jnp.dot(xb_ref[...], wr_ref[...], preferred_element_type=F32)
        if do_output:
            cols = slice(cs * out_cols, (cs + 1) * out_cols)
            y_ref[:, cols] += jnp.dot(mix_read[...], wo_b[cs], preferred_element_type=F32)
        if not do_elementwise:
            return

        rows = pl.ds(cs * nrow, nrow)
        strip = _strip_reader(p_read)
        pos = lax.broadcasted_iota(jnp.int32, (nrow, hd), 0) % steps
        per_row = lambda a: jnp.broadcast_to(a, (seqs, steps, hd)).reshape(nrow, hd)

        hist = hist_ref[...]
        prev2 = per_row(hist[:, 0:1, :])
        prev1 = per_row(hist[:, 1:2, :])
        u = strip(2, rows) * strip(0, rows)
        u1 = jnp.where(pos == 0, prev1, pltpu.roll(u, 1, axis=0))
        u2 = jnp.where(pos == 0, prev2, jnp.where(pos == 1, prev1, pltpu.roll(u, 2, axis=0)))
        ya = _conv_apply(u, u1, u2, cw_ref[...], strip(1, rows), strip(3, rows), na_ref[...])
        mix_write[rows, 0:hd] = ya.astype(BF16)
        cbuf_ref[...] = u.reshape(seqs, steps, hd)[:, steps - (CONV_WIDTH - 1):, :]

        lb = _lower_bound(lbl_ref[...])
        q, k, g = _gates(strip(4, rows), strip(5, rows), lb, hd)
        v = strip(6, rows)
        cum = _segment_cumsum(g, pos, steps)
        c_last = per_row(cum.reshape(seqs, steps, hd)[:, steps - 1:steps, :])
        q_e = q * jnp.exp(cum)
        k_e = k * jnp.exp(-cum)
        k_t = k * jnp.exp(c_last - cum)
        decay = jnp.exp(c_last)

        ri = lax.broadcasted_iota(jnp.int32, (nrow, nrow), 0)
        ci = lax.broadcasted_iota(jnp.int32, (nrow, nrow), 1)
        same_causal = (ri // steps == ci // steps) & (ri >= ci)
        scores = lax.dot_general(q_e.astype(BF16), k_e.astype(BF16), _NT,
                                 preferred_element_type=F32)
        probs = jnp.where(same_causal, scores, 0.0).astype(BF16)
        o_intra = jnp.dot(probs, v.astype(BF16), preferred_element_type=F32)

        o_inter = []
        for s in range(seqs):
            sr = slice(s * steps, (s + 1) * steps)
            s0 = s0_ref[s]
            o_inter.append(jnp.dot(q_e[sr].astype(BF16), s0.astype(BF16),
                                   preferred_element_type=F32))
            upd = lax.dot_general(k_t[sr].astype(BF16), v[sr].astype(BF16), _TN,
                                  preferred_element_type=F32)
            decay_col = jnp.broadcast_to(decay[s * steps:s * steps + 1], (hd, hd)).T
            snew_ref[s] = decay_col * s0 + upd
        o = o_intra + jnp.concatenate(o_inter, axis=0)

        yb = _lane_rms(o, nb_ref[...]) * _silu(strip(7, rows))
        mix_write[rows, hd:2 * hd] = yb.astype(BF16)

    steady = (jj >= 2) & (jj < pairs)
    for cs in range(COL_BLOCKS):
        at = c == cs
        _run_variants([(at & (jj == 0), 0, (True, False, False)),
                       (at & (jj == 1), 1, (True, True, False)),
                       (at & steady & (jj % 2 == 0), 0, (True, True, True)),
                       (at & steady & (jj % 2 == 1), 1, (True, True, True)),
                       (at & (jj == pairs), pairs % 2, (False, True, True)),
                       (at & (jj == pairs + 1), (pairs + 1) % 2, (False, False, True))],
                      functools.partial(stage, cs))

    @pl.when((jj == pairs + 1) & (c == pl.num_programs(1) - 1))
    def _():
        _layer_norm_rows(y_ref, lg_ref[...], lbias_ref[...], rows_all)


def kernel(x_prompt, x_sample, state_conv, state_hgrn, w_in, conv_w, norm_a, lb_logits,
           norm_b, w_out, ln_gain, ln_bias):
    batch, seq, d_model = x_prompt.shape
    dec_batch, dec_seq, _ = x_sample.shape
    depth, _, n_proj = w_in.shape
    assert depth == 1
    heads, dk, dv = state_hgrn.shape[2:]
    w_conv = state_conv.shape[-1]
    hd = LANES
    assert dk == hd and dv == hd and w_conv == heads * hd and n_proj == STRIPS * heads * hd
    assert dec_seq == SUBLANES and dec_batch // SAMPLE_SEQS == COL_BLOCKS
    assert seq % PROMPT_TILE == 0 and PROMPT_TILE % (PROMPT_COL_BLOCKS * CHUNK) == 0
    wide = PROMPT_COLS // MXU_COLS
    assert heads % PROMPT_GROUP == 0 and heads // PROMPT_GROUP > 2
    assert d_model % (COL_BLOCKS * LANES) == 0
    alpha = (2.0 * depth) ** 0.25
    f32 = x_prompt.dtype
    last = heads - 1

    def strip_spec(arr_rows, pair_of):
        return pl.BlockSpec((arr_rows, hd), lambda *g: (0, pair_of(*g)))

    rows_s = dec_batch * dec_seq
    chunks = dec_batch // SAMPLE_SEQS
    xs = x_sample.reshape(rows_s, d_model)
    mm_s = lambda jj, c: jnp.minimum(jj, last)
    ew_s = lambda jj, c: jnp.clip(jj - 1, 0, last)
    out_s = lambda jj, c: jnp.clip(jj - 2, 0, last)
    mm_c = lambda jj, c: jnp.where(jj > last, chunks - 1, c)
    ew_c = lambda jj, c: jnp.where(jj == 0, 0, jnp.where(jj > heads, chunks - 1, c))
    resident = lambda shape: pl.BlockSpec(shape, lambda jj, c: (0, 0),
                                          pipeline_mode=pl.Buffered(1))
    state_spec = pl.BlockSpec((None, SAMPLE_SEQS, None, hd, hd),
                              lambda jj, c: (0, ew_c(jj, c), ew_s(jj, c), 0, 0))
    hist_spec = pl.BlockSpec((None, SAMPLE_SEQS, CONV_WIDTH - 1, hd),
                             lambda jj, c: (0, ew_c(jj, c), 0, ew_s(jj, c)))
    w_strip = lambda half: pl.BlockSpec(
        (d_model, hd), lambda jj, c: (0, (2 * mm_c(jj, c) + half) * heads + mm_s(jj, c)))
    wo_strip = lambda half: pl.BlockSpec(
        (hd, d_model), lambda jj, c: (half * heads + out_s(jj, c), 0))
    y_s, conv_s, hgrn_s, w_r, wo_r = pl.pallas_call(
        functools.partial(_sample_body, seqs=SAMPLE_SEQS, steps=dec_seq, pairs=heads,
                          alpha=alpha),
        grid=(heads + 2, chunks),
        in_specs=[
            resident((rows_s, d_model)),
            hist_spec,
            state_spec,
            w_strip(0), w_strip(1), wo_strip(0), wo_strip(1),
            pl.BlockSpec((None, CONV_WIDTH, hd), lambda jj, c: (0, 0, ew_s(jj, c))),
            strip_spec(1, ew_s), strip_spec(lb_logits.shape[0], ew_s), strip_spec(1, ew_s),
            pl.BlockSpec((1, d_model), lambda jj, c: (0, 0)),
            pl.BlockSpec((1, d_model), lambda jj, c: (0, 0)),
        ],
        out_specs=[
            resident((rows_s, d_model)),
            hist_spec,
            state_spec,
            pl.BlockSpec((None, None, d_model, MXU_COLS),
                         lambda jj, c: (mm_s(jj, c), mm_c(jj, c) // wide, 0, mm_c(jj, c) % wide)),
            pl.BlockSpec((None, 2 * hd, d_model), lambda jj, c: (out_s(jj, c), 0, 0)),
        ],
        out_shape=[
            jax.ShapeDtypeStruct((rows_s, d_model), f32),
            jax.ShapeDtypeStruct((depth, dec_batch, CONV_WIDTH - 1, w_conv), state_conv.dtype),
            jax.ShapeDtypeStruct((depth, dec_batch, heads, dk, dv), state_hgrn.dtype),
            jax.ShapeDtypeStruct((heads, PROMPT_COL_BLOCKS, d_model, PROMPT_COLS), BF16),
            jax.ShapeDtypeStruct((heads, 2 * hd, d_model), BF16),
        ],
        scratch_shapes=[
            pltpu.VMEM((rows_s, d_model), BF16),
            pltpu.VMEM((COL_BLOCKS, rows_s, MXU_COLS), F32),
            pltpu.VMEM((COL_BLOCKS, rows_s, MXU_COLS), F32),
            pltpu.VMEM((rows_s, 2 * hd), BF16),
            pltpu.VMEM((rows_s, 2 * hd), BF16),
            pltpu.VMEM((COL_BLOCKS, 2 * hd, d_model // COL_BLOCKS), BF16),
        ],
        compiler_params=pltpu.CompilerParams(
            dimension_semantics=("arbitrary", "arbitrary"),
            vmem_limit_bytes=VMEM_LIMIT),
        name="sample_mixer",
    )(xs, state_conv, state_hgrn, w_in[0], w_in[0], w_out[0], w_out[0],
      conv_w, norm_a, lb_logits, norm_b, ln_gain, ln_bias)

    tile = PROMPT_TILE
    tiles_per_seq = seq // tile
    group = PROMPT_GROUP
    per_tile = heads // group
    units = batch * tiles_per_seq * per_tile
    unit_p = lambda n: jnp.minimum(n, units - 1)
    unit_e = lambda n: jnp.clip(n - 1, 0, units - 1)
    unit_o = lambda n: jnp.clip(n - 2, 0, units - 1)
    group_p = lambda n: unit_p(n) % per_tile
    group_e = lambda n: unit_e(n) % per_tile
    group_o = lambda n: unit_o(n) % per_tile
    batch_e = lambda n: unit_e(n) // (per_tile * tiles_per_seq)

    def row_tile_map(unit_of):
        def index_map(n):
            row_tile = unit_of(n) // per_tile
            return (row_tile // tiles_per_seq, row_tile % tiles_per_seq, 0)
        return index_map

    def group_strip(arr_rows):
        return pl.BlockSpec((arr_rows, group * hd), lambda n: (0, group_e(n)))

    y_p, conv_p, hgrn_p = pl.pallas_call(
        functools.partial(_prompt_body, tile=tile, tiles_per_seq=tiles_per_seq, pairs=heads,
                          group=group, units=units, alpha=alpha),
        grid=(units + 2,),
        in_specs=[
            pl.BlockSpec((None, tile, d_model), row_tile_map(unit_p)),
            pl.BlockSpec((group, PROMPT_COL_BLOCKS, d_model, PROMPT_COLS),
                         lambda n: (group_p(n), 0, 0, 0)),
            pl.BlockSpec((group * 2 * hd, d_model), lambda n: (group_o(n), 0)),
            pl.BlockSpec((None, CONV_WIDTH, group * hd), lambda n: (0, 0, group_e(n))),
            group_strip(1), group_strip(lb_logits.shape[0]), group_strip(1),
            pl.BlockSpec((1, d_model), lambda n: (0, 0)),
            pl.BlockSpec((1, d_model), lambda n: (0, 0)),
        ],
        out_specs=[
            pl.BlockSpec((None, tile, d_model), row_tile_map(unit_o)),
            pl.BlockSpec((None, None, CONV_WIDTH - 1, w_conv), lambda n: (0, batch_e(n), 0, 0)),
            pl.BlockSpec((None, None, heads, hd, hd), lambda n: (0, batch_e(n), 0, 0, 0)),
        ],
        out_shape=[
            jax.ShapeDtypeStruct((batch, seq, d_model), f32),
            jax.ShapeDtypeStruct((depth, batch, CONV_WIDTH - 1, w_conv), state_conv.dtype),
            jax.ShapeDtypeStruct((depth, batch, heads, dk, dv), state_hgrn.dtype),
        ],
        scratch_shapes=[
            pltpu.VMEM((tile, d_model), BF16),
            pltpu.VMEM((group, PROMPT_COL_BLOCKS, tile, PROMPT_COLS), F32),
            pltpu.VMEM((group, PROMPT_COL_BLOCKS, tile, PROMPT_COLS), F32),
            pltpu.VMEM((tile, group * 2 * hd), BF16),
            pltpu.VMEM((tile, group * 2 * hd), BF16),
            pltpu.VMEM((heads, hd, hd), F32),
            pltpu.VMEM((heads, SUBLANES, hd), F32),
        ],
        compiler_params=pltpu.CompilerParams(
            dimension_semantics=("arbitrary",),
            vmem_limit_bytes=VMEM_LIMIT),
        name="prompt_mixer",
    )(x_prompt, w_r, wo_r.reshape(heads * 2 * hd, d_model), conv_w, norm_a, lb_logits, norm_b,
      ln_gain, ln_bias)

    return (y_p, y_s.reshape(x_sample.shape), conv_p, hgrn_p, conv_s, hgrn_s)
```

```python
import functools

import jax
import jax.numpy as jnp
from jax import lax
from jax.experimental import pallas as pl
from jax.experimental.pallas import tpu as pltpu

LANES = 128
SUBLANES = 8
MXU_COLS = 256
STRIPS = 8
COL_BLOCKS = STRIPS * LANES // MXU_COLS
PROMPT_COLS = 2 * MXU_COLS
PROMPT_COL_BLOCKS = STRIPS * LANES // PROMPT_COLS
CONV_WIDTH = 3
EPS = 1e-5
PROMPT_TILE = 512
PROMPT_GROUP = 2
CHUNK = 128
SAMPLE_SEQS = 32
VMEM_LIMIT = 56 * 1024 * 1024

F32 = jnp.float32
BF16 = jnp.bfloat16
_NT = (((1,), (1,)), ((), ()))
_TN = (((0,), (0,)), ((), ()))


def _sigmoid(x):
    return 1.0 / (1.0 + jnp.exp(-x))


def _silu(x):
    return x * _sigmoid(x)


def _lane_rms(x, gain):
    return x * lax.rsqrt(jnp.mean(x * x, axis=-1, keepdims=True) + EPS) * gain


def _lower_bound(lbl):
    e = jnp.exp(lbl - jnp.max(lbl, axis=0, keepdims=True))
    return e[0:1] / jnp.sum(e, axis=0, keepdims=True)


def _segment_cumsum(g, pos, seg):
    s = 1
    while s < min(seg, SUBLANES):
        g = g + jnp.where(pos >= s, pltpu.roll(g, s, axis=0), 0.0)
        s *= 2
    while s < seg:
        assert g.shape[0] == seg
        g = jnp.concatenate([g[:s], g[s:] + g[:-s]], axis=0)
        s *= 2
    return g


def _layer_norm_rows(y_ref, gain, bias, rows):
    def body(i, _):
        r0 = pl.multiple_of(i * CHUNK, CHUNK)
        z = y_ref[pl.ds(r0, CHUNK), :]
        mu = jnp.mean(z, axis=-1, keepdims=True)
        zc = z - mu
        var = jnp.mean(zc * zc, axis=-1, keepdims=True)
        y_ref[pl.ds(r0, CHUNK), :] = zc * lax.rsqrt(var + EPS) * gain + bias
        return 0
    lax.fori_loop(0, rows // CHUNK, body, 0)


def _cast_rows(x_ref, xb_ref, rows):
    def body(i, _):
        r0 = pl.multiple_of(i * CHUNK, CHUNK)
        xb_ref[pl.ds(r0, CHUNK), :] = x_ref[pl.ds(r0, CHUNK), :].astype(BF16)
        return 0
    lax.fori_loop(0, rows // CHUNK, body, 0)


def _scale_rows(x_ref, y_ref, alpha, rows):
    def body(i, _):
        r0 = pl.multiple_of(i * CHUNK, CHUNK)
        y_ref[pl.ds(r0, CHUNK), :] = alpha * x_ref[pl.ds(r0, CHUNK), :]
        return 0
    lax.fori_loop(0, rows // CHUNK, body, 0)


def _gates(qb, fb, lb, dk):
    q = _silu(qb) * (dk ** -0.5)
    f = lb + (1.0 - lb) * _sigmoid(fb)
    return q, 1.0 - f, jnp.log(f)


def _strip_reader(p_ref):
    def strip(k, rows):
        lo = (k % 2) * LANES
        return p_ref[k // 2, rows, lo:lo + LANES]
    return strip


def _conv_apply(u, u1, u2, cw, b_a, z_a, na):
    conv = cw[0:1] * u2 + cw[1:2] * u1 + cw[2:3] * u
    return _lane_rms(b_a * conv, na) * _silu(z_a)


def _run_variants(variants, stage):
    for cond, parity, flags in variants:
        pl.when(cond)(functools.partial(stage, parity, *flags))


def _prompt_body(x_ref, w_ref, wo_ref, cw_ref, na_ref, lbl_ref, nb_ref, lg_ref, lbias_ref,
                 y_ref, cbuf_ref, snew_ref,
                 xb_ref, p_even, p_odd, mix_even, mix_odd, st_ref, cv_ref,
                 *, tile, tiles_per_seq, pairs, group, units, alpha):
    n = pl.program_id(0)
    hd = LANES
    chunks_per_block = tile // (PROMPT_COL_BLOCKS * CHUNK)
    strips_per_block = PROMPT_COLS // hd
    per_tile = pairs // group
    unit_e = jnp.clip(n - 1, 0, units - 1)
    j0 = (unit_e % per_tile) * group
    row_tile = unit_e // per_tile
    t = row_tile % tiles_per_seq

    @pl.when((n % per_tile == 0) & (n < units))
    def _():
        _cast_rows(x_ref, xb_ref, tile)

    @pl.when(n % per_tile == 2)
    def _():
        _scale_rows(x_ref, y_ref, alpha, tile)

    @pl.when((n >= 1) & (j0 == 0) & (t == 0))
    def _():
        st_ref[...] = jnp.zeros_like(st_ref)
        cv_ref[...] = jnp.zeros_like(cv_ref)

    def stage(parity, do_project, do_elementwise, do_output):
        p_write, p_read = (p_even, p_odd) if parity == 0 else (p_odd, p_even)
        mix_write, mix_read = (mix_even, mix_odd) if parity == 0 else (mix_odd, mix_even)

        def project(g, i):
            p_write[g, i] = jnp.dot(xb_ref[...], w_ref[g, i], preferred_element_type=F32)

        def output():
            y_ref[...] += jnp.dot(mix_read[...], wo_ref[...], preferred_element_type=F32)

        if not do_elementwise:
            if do_project:
                def project_block(gi, _):
                    project(gi // PROMPT_COL_BLOCKS, gi % PROMPT_COL_BLOCKS)
                    return 0
                lax.fori_loop(0, group * PROMPT_COL_BLOCKS, project_block, 0)
            if do_output:
                output()
            return
        pos = lax.broadcasted_iota(jnp.int32, (CHUNK, hd), 0)
        causal = (lax.broadcasted_iota(jnp.int32, (CHUNK, CHUNK), 0)
                  >= lax.broadcasted_iota(jnp.int32, (CHUNK, CHUNK), 1))
        mid = CHUNK // 2 - 1
        tails = [elementwise_unit(g, p_write, p_read, mix_write, do_project, pos, causal, mid)
                 for g in range(group)]
        if do_output:
            output()

        @pl.when(t == tiles_per_seq - 1)
        def _():
            for g in range(group):
                lanes = pl.ds(pl.multiple_of((j0 + g) * hd, hd), hd)
                cbuf_ref[:, lanes] = tails[g][SUBLANES - (CONV_WIDTH - 1):SUBLANES]
                snew_ref[j0 + g] = st_ref[j0 + g].T

    def elementwise_unit(g, p_write, p_read, mix_write, do_project, pos, causal, mid):
        j = j0 + g
        lanes = slice(g * hd, (g + 1) * hd)
        mix_lo = g * 2 * hd

        def strip(k, rows):
            lo = (k % strips_per_block) * hd
            return p_read[g, k // strips_per_block, rows, lo:lo + hd]

        cw = cw_ref[:, lanes]
        na = na_ref[:, lanes]
        nb = nb_ref[:, lanes]
        lb = _lower_bound(lbl_ref[:, lanes])

        def conv_chunk(rows, tail):
            u = strip(2, rows) * strip(0, rows)
            head = (SUBLANES, hd)
            prev1 = jnp.broadcast_to(tail[SUBLANES - 1:SUBLANES], head)
            prev2 = jnp.broadcast_to(tail[SUBLANES - 2:SUBLANES - 1], head)
            pos8 = pos[:SUBLANES]
            u1 = pltpu.roll(u, 1, axis=0)
            u1 = jnp.concatenate([jnp.where(pos8 == 0, prev1, u1[:SUBLANES]), u1[SUBLANES:]], axis=0)
            u2 = pltpu.roll(u, 2, axis=0)
            u2_head = jnp.where(pos8 == 0, prev2, jnp.where(pos8 == 1, prev1, u2[:SUBLANES]))
            u2 = jnp.concatenate([u2_head, u2[SUBLANES:]], axis=0)
            ya = _conv_apply(u, u1, u2, cw, strip(1, rows), strip(3, rows), na)
            mix_write[rows, mix_lo:mix_lo + hd] = ya.astype(BF16)
            return u[CHUNK - SUBLANES:CHUNK]

        def hgrn_chunk(rows):
            q, k, log_f = _gates(strip(4, rows), strip(5, rows), lb, hd)
            v = strip(6, rows)
            cum = _segment_cumsum(log_f, pos, CHUNK)
            c_mid = cum[mid:mid + 1]
            c_last = cum[CHUNK - 1:CHUNK]
            q_e = q * jnp.exp(cum - c_mid)
            k_e = k * jnp.exp(c_mid - cum)
            k_t = k_e * jnp.exp(c_last - c_mid)
            q_i = q_e * jnp.exp(c_mid)
            s_t = st_ref[j]
            scores = lax.dot_general(q_e.astype(BF16), k_e.astype(BF16), _NT,
                                     preferred_element_type=F32)
            probs = jnp.where(causal, scores, 0.0).astype(BF16)
            o = jnp.dot(probs, v.astype(BF16), preferred_element_type=F32)
            o = o + lax.dot_general(q_i.astype(BF16), s_t.astype(BF16), _NT,
                                    preferred_element_type=F32)
            st_ref[j] = jnp.exp(c_last) * s_t + jnp.dot(
                v.T.astype(BF16), k_t.astype(BF16), preferred_element_type=F32)
            yb = _lane_rms(o, nb) * _silu(strip(7, rows))
            mix_write[rows, mix_lo + hd:mix_lo + 2 * hd] = yb.astype(BF16)

        tail = cv_ref[j]
        for i in range(PROMPT_COL_BLOCKS):
            if do_project:
                p_write[g, i] = jnp.dot(xb_ref[...], w_ref[g, i], preferred_element_type=F32)
            for c in range(chunks_per_block):
                rows = pl.ds((i * chunks_per_block + c) * CHUNK, CHUNK)
                tail = conv_chunk(rows, tail)
                hgrn_chunk(rows)
        cv_ref[j] = tail
        return tail

    steady = (n >= 2) & (n < units)
    _run_variants([(n == 0, 0, (True, False, False)),
                   (n == 1, 1, (True, True, False)),
                   (steady & (n % 2 == 0), 0, (True, True, True)),
                   (steady & (n % 2 == 1), 1, (True, True, True)),
                   (n == units, units % 2, (False, True, True)),
                   (n == units + 1, (units + 1) % 2, (False, False, True))], stage)

    @pl.when((n >= 2) & ((n - 2) % per_tile == per_tile - 1))
    def _():
        _layer_norm_rows(y_ref, lg_ref[...], lbias_ref[...], tile)


def _sample_body(x_ref, hist_ref, s0_ref, wa_ref, wb_ref, woa_ref, wob_ref,
                 cw_ref, na_ref, lbl_ref, nb_ref, lg_ref, lbias_ref,
                 y_ref, cbuf_ref, snew_ref, wr_ref, wor_ref,
                 xb_ref, p_even, p_odd, mix_even, mix_odd, wo_b,
                 *, seqs, steps, pairs, alpha):
    jj = pl.program_id(0)
    c = pl.program_id(1)
    hd = LANES
    rows_all, d_model = x_ref.shape
    nrow = seqs * steps
    out_cols = d_model // COL_BLOCKS

    @pl.when((jj == 0) & (c == 0))
    def _():
        _cast_rows(x_ref, xb_ref, rows_all)
        _scale_rows(x_ref, y_ref, alpha, rows_all)

    @pl.when((jj >= 2) & (c == 0))
    def _():
        for ref, lo in ((woa_ref, 0), (wob_ref, hd)):
            wor_ref[lo:lo + hd, :] = ref[...].astype(BF16)
            for q in range(COL_BLOCKS):
                wo_b[q, lo:lo + hd, :] = ref[:, q * out_cols:(q + 1) * out_cols].astype(BF16)

    def stage(parity, do_project, do_elementwise, do_output):
        p_write, p_read = (p_even, p_odd) if parity == 0 else (p_odd, p_even)
        mix_write, mix_read = (mix_even, mix_odd) if parity == 0 else (mix_odd, mix_even)

        if do_project:
            wr_ref[:, 0:hd] = wa_ref[...].astype(BF16)
            wr_ref[:, hd:2 * hd] = wb_ref[...].astype(BF16)
            p_write[c] = jnp.dot(xb_ref[...], wr_ref[...], preferred_element_type=F32)
        if do_output:
            cols = pl.ds(pl.multiple_of(c * out_cols, out_cols), out_cols)
            y_ref[:, cols] += jnp.dot(mix_read[...], wo_b[c], preferred_element_type=F32)
        if not do_elementwise:
            return

        rows = pl.ds(pl.multiple_of(c * nrow, nrow), nrow)
        strip = _strip_reader(p_read)
        pos = lax.broadcasted_iota(jnp.int32, (nrow, hd), 0) % steps
        per_row = lambda a: jnp.broadcast_to(a, (seqs, steps, hd)).reshape(nrow, hd)

        hist = hist_ref[...]
        prev2 = per_row(hist[:, 0:1, :])
        prev1 = per_row(hist[:, 1:2, :])
        u = strip(2, rows) * strip(0, rows)
        u1 = jnp.where(pos == 0, prev1, pltpu.roll(u, 1, axis=0))
        u2 = jnp.where(pos == 0, prev2, jnp.where(pos == 1, prev1, pltpu.roll(u, 2, axis=0)))
        ya = _conv_apply(u, u1, u2, cw_ref[...], strip(1, rows), strip(3, rows), na_ref[...])
        mix_write[rows, 0:hd] = ya.astype(BF16)
        cbuf_ref[...] = u.reshape(seqs, steps, hd)[:, steps - (CONV_WIDTH - 1):, :]

        lb = _lower_bound(lbl_ref[...])
        q, k, g = _gates(strip(4, rows), strip(5, rows), lb, hd)
        v = strip(6, rows)
        cum = _segment_cumsum(g, pos, steps)
        c_last = per_row(cum.reshape(seqs, steps, hd)[:, steps - 1:steps, :])
        q_e = q * jnp.exp(cum)
        k_e = k * jnp.exp(-cum)
        k_t = k * jnp.exp(c_last - cum)
        decay = jnp.exp(c_last)

        ri = lax.broadcasted_iota(jnp.int32, (nrow, nrow), 0)
        ci = lax.broadcasted_iota(jnp.int32, (nrow, nrow), 1)
        same_causal = (ri // steps == ci // steps) & (ri >= ci)
        scores = lax.dot_general(q_e.astype(BF16), k_e.astype(BF16), _NT,
                                 preferred_element_type=F32)
        probs = jnp.where(same_causal, scores, 0.0).astype(BF16)
        o_intra = jnp.dot(probs, v.astype(BF16), preferred_element_type=F32)

        o_inter = []
        for s in range(seqs):
            sr = slice(s * steps, (s + 1) * steps)
            s0 = s0_ref[s]
            o_inter.append(jnp.dot(q_e[sr].astype(BF16), s0.astype(BF16),
                                   preferred_element_type=F32))
            upd = lax.dot_general(k_t[sr].astype(BF16), v[sr].astype(BF16), _TN,
                                  preferred_element_type=F32)
            decay_col = jnp.broadcast_to(decay[s * steps:s * steps + 1], (hd, hd)).T
            snew_ref[s] = decay_col * s0 + upd
        o = o_intra + jnp.concatenate(o_inter, axis=0)

        yb = _lane_rms(o, nb_ref[...]) * _silu(strip(7, rows))
        mix_write[rows, hd:2 * hd] = yb.astype(BF16)

    steady = (jj >= 2) & (jj < pairs)
    _run_variants([(jj == 0, 0, (True, False, False)),
                   (jj == 1, 1, (True, True, False)),
                   (steady & (jj % 2 == 0), 0, (True, True, True)),
                   (steady & (jj % 2 == 1), 1, (True, True, True)),
                   (jj == pairs, pairs % 2, (False, True, True)),
                   (jj == pairs + 1, (pairs + 1) % 2, (False, False, True))], stage)

    @pl.when((jj == pairs + 1) & (c == pl.num_programs(1) - 1))
    def _():
        _layer_norm_rows(y_ref, lg_ref[...], lbias_ref[...], rows_all)


def kernel(x_prompt, x_sample, state_conv, state_hgrn, w_in, conv_w, norm_a, lb_logits,
           norm_b, w_out, ln_gain, ln_bias):
    batch, seq, d_model = x_prompt.shape
    dec_batch, dec_seq, _ = x_sample.shape
    depth, _, n_proj = w_in.shape
    assert depth == 1
    heads, dk, dv = state_hgrn.shape[2:]
    w_conv = state_conv.shape[-1]
    hd = LANES
    assert dk == hd and dv == hd and w_conv == heads * hd and n_proj == STRIPS * heads * hd
    assert dec_seq == SUBLANES and dec_batch // SAMPLE_SEQS == COL_BLOCKS
    assert seq % PROMPT_TILE == 0 and PROMPT_TILE % (PROMPT_COL_BLOCKS * CHUNK) == 0
    wide = PROMPT_COLS // MXU_COLS
    assert heads % PROMPT_GROUP == 0 and heads // PROMPT_GROUP > 2
    assert d_model % (COL_BLOCKS * LANES) == 0
    alpha = (2.0 * depth) ** 0.25
    f32 = x_prompt.dtype
    last = heads - 1

    def strip_spec(arr_rows, pair_of):
        return pl.BlockSpec((arr_rows, hd), lambda *g: (0, pair_of(*g)))

    rows_s = dec_batch * dec_seq
    chunks = dec_batch // SAMPLE_SEQS
    xs = x_sample.reshape(rows_s, d_model)
    mm_s = lambda jj, c: jnp.minimum(jj, last)
    ew_s = lambda jj, c: jnp.clip(jj - 1, 0, last)
    out_s = lambda jj, c: jnp.clip(jj - 2, 0, last)
    mm_c = lambda jj, c: jnp.where(jj > last, chunks - 1, c)
    ew_c = lambda jj, c: jnp.where(jj == 0, 0, jnp.where(jj > heads, chunks - 1, c))
    resident = lambda shape: pl.BlockSpec(shape, lambda jj, c: (0, 0),
                                          pipeline_mode=pl.Buffered(1))
    state_spec = pl.BlockSpec((None, SAMPLE_SEQS, None, hd, hd),
                              lambda jj, c: (0, ew_c(jj, c), ew_s(jj, c), 0, 0))
    hist_spec = pl.BlockSpec((None, SAMPLE_SEQS, CONV_WIDTH - 1, hd),
                             lambda jj, c: (0, ew_c(jj, c), 0, ew_s(jj, c)))
    w_strip = lambda half: pl.BlockSpec(
        (d_model, hd), lambda jj, c: (0, (2 * mm_c(jj, c) + half) * heads + mm_s(jj, c)))
    wo_strip = lambda half: pl.BlockSpec(
        (hd, d_model), lambda jj, c: (half * heads + out_s(jj, c), 0))
    y_s, conv_s, hgrn_s, w_r, wo_r = pl.pallas_call(
        functools.partial(_sample_body, seqs=SAMPLE_SEQS, steps=dec_seq, pairs=heads,
                          alpha=alpha),
        grid=(heads + 2, chunks),
        in_specs=[
            resident((rows_s, d_model)),
            hist_spec,
            state_spec,
            w_strip(0), w_strip(1), wo_strip(0), wo_strip(1),
            pl.BlockSpec((None, CONV_WIDTH, hd), lambda jj, c: (0, 0, ew_s(jj, c))),
            strip_spec(1, ew_s), strip_spec(lb_logits.shape[0], ew_s), strip_spec(1, ew_s),
            pl.BlockSpec((1, d_model), lambda jj, c: (0, 0)),
            pl.BlockSpec((1, d_model), lambda jj, c: (0, 0)),
        ],
        out_specs=[
            resident((rows_s, d_model)),
            hist_spec,
            state_spec,
            pl.BlockSpec((None, None, d_model, MXU_COLS),
                         lambda jj, c: (mm_s(jj, c), mm_c(jj, c) // wide, 0, mm_c(jj, c) % wide)),
            pl.BlockSpec((None, 2 * hd, d_model), lambda jj, c: (out_s(jj, c), 0, 0)),
        ],
        out_shape=[
            jax.ShapeDtypeStruct((rows_s, d_model), f32),
            jax.ShapeDtypeStruct((depth, dec_batch, CONV_WIDTH - 1, w_conv), state_conv.dtype),
            jax.ShapeDtypeStruct((depth, dec_batch, heads, dk, dv), state_hgrn.dtype),
            jax.ShapeDtypeStruct((heads, PROMPT_COL_BLOCKS, d_model, PROMPT_COLS), BF16),
            jax.ShapeDtypeStruct((heads, 2 * hd, d_model), BF16),
        ],
        scratch_shapes=[
            pltpu.VMEM((rows_s, d_model), BF16),
            pltpu.VMEM((COL_BLOCKS, rows_s, MXU_COLS), F32),
            pltpu.VMEM((COL_BLOCKS, rows_s, MXU_COLS), F32),
            pltpu.VMEM((rows_s, 2 * hd), BF16),
            pltpu.VMEM((rows_s, 2 * hd), BF16),
            pltpu.VMEM((COL_BLOCKS, 2 * hd, d_model // COL_BLOCKS), BF16),
        ],
        compiler_params=pltpu.CompilerParams(
            dimension_semantics=("arbitrary", "arbitrary"),
            vmem_limit_bytes=VMEM_LIMIT),
        name="sample_mixer",
    )(xs, state_conv, state_hgrn, w_in[0], w_in[0], w_out[0], w_out[0],
      conv_w, norm_a, lb_logits, norm_b, ln_gain, ln_bias)

    tile = PROMPT_TILE
    tiles_per_seq = seq // tile
    group = PROMPT_GROUP
    per_tile = heads // group
    units = batch * tiles_per_seq * per_tile
    unit_p = lambda n: jnp.minimum(n, units - 1)
    unit_e = lambda n: jnp.clip(n - 1, 0, units - 1)
    unit_o = lambda n: jnp.clip(n - 2, 0, units - 1)
    group_p = lambda n: unit_p(n) % per_tile
    group_e = lambda n: unit_e(n) % per_tile
    group_o = lambda n: unit_o(n) % per_tile
    batch_e = lambda n: unit_e(n) // (per_tile * tiles_per_seq)

    def row_tile_map(unit_of):
        def index_map(n):
            row_tile = unit_of(n) // per_tile
            return (row_tile // tiles_per_seq, row_tile % tiles_per_seq, 0)
        return index_map

    def group_strip(arr_rows):
        return pl.BlockSpec((arr_rows, group * hd), lambda n: (0, group_e(n)))

    y_p, conv_p, hgrn_p = pl.pallas_call(
        functools.partial(_prompt_body, tile=tile, tiles_per_seq=tiles_per_seq, pairs=heads,
                          group=group, units=units, alpha=alpha),
        grid=(units + 2,),
        in_specs=[
            pl.BlockSpec((None, tile, d_model), row_tile_map(unit_p)),
            pl.BlockSpec((group, PROMPT_COL_BLOCKS, d_model, PROMPT_COLS),
                         lambda n: (group_p(n), 0, 0, 0)),
            pl.BlockSpec((group * 2 * hd, d_model), lambda n: (group_o(n), 0)),
            pl.BlockSpec((None, CONV_WIDTH, group * hd), lambda n: (0, 0, group_e(n))),
            group_strip(1), group_strip(lb_logits.shape[0]), group_strip(1),
            pl.BlockSpec((1, d_model), lambda n: (0, 0)),
            pl.BlockSpec((1, d_model), lambda n: (0, 0)),
        ],
        out_specs=[
            pl.BlockSpec((None, tile, d_model), row_tile_map(unit_o)),
            pl.BlockSpec((None, None, CONV_WIDTH - 1, w_conv), lambda n: (0, batch_e(n), 0, 0)),
            pl.BlockSpec((None, None, heads, hd, hd), lambda n: (0, batch_e(n), 0, 0, 0)),
        ],
        out_shape=[
            jax.ShapeDtypeStruct((batch, seq, d_model), f32),
            jax.ShapeDtypeStruct((depth, batch, CONV_WIDTH - 1, w_conv), state_conv.dtype),
            jax.ShapeDtypeStruct((depth, batch, heads, dk, dv), state_hgrn.dtype),
        ],
        scratch_shapes=[
            pltpu.VMEM((tile, d_model), BF16),
            pltpu.VMEM((group, PROMPT_COL_BLOCKS, tile, PROMPT_COLS), F32),
            pltpu.VMEM((group, PROMPT_COL_BLOCKS, tile, PROMPT_COLS), F32),
            pltpu.VMEM((tile, group * 2 * hd), BF16),
            pltpu.VMEM((tile, group * 2 * hd), BF16),
            pltpu.VMEM((heads, hd, hd), F32),
            pltpu.VMEM((heads, SUBLANES, hd), F32),
        ],
        compiler_params=pltpu.CompilerParams(
            dimension_semantics=("arbitrary",),
            vmem_limit_bytes=VMEM_LIMIT),
        name="prompt_mixer",
    )(x_prompt, w_r, wo_r.reshape(heads * 2 * hd, d_model), conv_w, norm_a, lb_logits, norm_b,
      ln_gain, ln_bias)

    return (y_p, y_s.reshape(x_sample.shape), conv_p, hgrn_p, conv_s, hgrn_s)
```

```python
import functools

import jax
import jax.numpy as jnp
from jax import lax
from jax.experimental import pallas as pl
from jax.experimental.pallas import tpu as pltpu

LANES = 128
SUBLANES = 8
MXU_COLS = 256
STRIPS = 8
COL_BLOCKS = STRIPS * LANES // MXU_COLS
PROMPT_COLS = 2 * MXU_COLS
PROMPT_COL_BLOCKS = STRIPS * LANES // PROMPT_COLS
CONV_WIDTH = 3
EPS = 1e-5
PROMPT_TILE = 512
PROMPT_GROUP = 2
CHUNK = 128
SAMPLE_SEQS = 32
VMEM_LIMIT = 56 * 1024 * 1024

F32 = jnp.float32
BF16 = jnp.bfloat16
_NT = (((1,), (1,)), ((), ()))
_TN = (((0,), (0,)), ((), ()))


def _sigmoid(x):
    return 1.0 / (1.0 + jnp.exp(-x))


def _silu(x):
    return x * _sigmoid(x)


def _lane_rms(x, gain):
    return x * lax.rsqrt(jnp.mean(x * x, axis=-1, keepdims=True) + EPS) * gain


def _lower_bound(lbl):
    e = jnp.exp(lbl - jnp.max(lbl, axis=0, keepdims=True))
    return e[0:1] / jnp.sum(e, axis=0, keepdims=True)


def _segment_cumsum(g, pos, seg):
    s = 1
    while s < min(seg, SUBLANES):
        g = g + jnp.where(pos >= s, pltpu.roll(g, s, axis=0), 0.0)
        s *= 2
    while s < seg:
        assert g.shape[0] == seg
        g = jnp.concatenate([g[:s], g[s:] + g[:-s]], axis=0)
        s *= 2
    return g


def _layer_norm_rows(y_ref, gain, bias, rows):
    def body(i, _):
        r0 = pl.multiple_of(i * CHUNK, CHUNK)
        z = y_ref[pl.ds(r0, CHUNK), :]
        mu = jnp.mean(z, axis=-1, keepdims=True)
        zc = z - mu
        var = jnp.mean(zc * zc, axis=-1, keepdims=True)
        y_ref[pl.ds(r0, CHUNK), :] = zc * lax.rsqrt(var + EPS) * gain + bias
        return 0
    lax.fori_loop(0, rows // CHUNK, body, 0)


def _cast_rows(x_ref, xb_ref, rows):
    def body(i, _):
        r0 = pl.multiple_of(i * CHUNK, CHUNK)
        xb_ref[pl.ds(r0, CHUNK), :] = x_ref[pl.ds(r0, CHUNK), :].astype(BF16)
        return 0
    lax.fori_loop(0, rows // CHUNK, body, 0)


def _scale_rows(x_ref, y_ref, alpha, rows):
    def body(i, _):
        r0 = pl.multiple_of(i * CHUNK, CHUNK)
        y_ref[pl.ds(r0, CHUNK), :] = alpha * x_ref[pl.ds(r0, CHUNK), :]
        return 0
    lax.fori_loop(0, rows // CHUNK, body, 0)


def _gates(qb, fb, lb, dk):
    q = _silu(qb) * (dk ** -0.5)
    f = lb + (1.0 - lb) * _sigmoid(fb)
    return q, 1.0 - f, jnp.log(f)


def _strip_reader(p_ref):
    def strip(k, rows):
        lo = (k % 2) * LANES
        return p_ref[k // 2, rows, lo:lo + LANES]
    return strip


def _conv_apply(u, u1, u2, cw, b_a, z_a, na):
    conv = cw[0:1] * u2 + cw[1:2] * u1 + cw[2:3] * u
    return _lane_rms(b_a * conv, na) * _silu(z_a)


def _run_variants(variants, stage):
    for cond, parity, flags in variants:
        pl.when(cond)(functools.partial(stage, parity, *flags))


def _prompt_body(x_ref, w_ref, wo_ref, cw_ref, na_ref, lbl_ref, nb_ref, lg_ref, lbias_ref,
                 y_ref, cbuf_ref, snew_ref,
                 xb_ref, p_even, p_odd, mix_even, mix_odd, st_ref, cv_ref,
                 *, tile, tiles_per_seq, pairs, group, units, alpha):
    n = pl.program_id(0)
    hd = LANES
    chunks_per_block = tile // (PROMPT_COL_BLOCKS * CHUNK)
    strips_per_block = PROMPT_COLS // hd
    per_tile = pairs // group
    unit_e = jnp.clip(n - 1, 0, units - 1)
    j0 = (unit_e % per_tile) * group
    row_tile = unit_e // per_tile
    t = row_tile % tiles_per_seq

    @pl.when((n % per_tile == 0) & (n < units))
    def _():
        _cast_rows(x_ref, xb_ref, tile)

    @pl.when(n % per_tile == 2)
    def _():
        _scale_rows(x_ref, y_ref, alpha, tile)

    @pl.when((n >= 1) & (j0 == 0) & (t == 0))
    def _():
        st_ref[...] = jnp.zeros_like(st_ref)
        cv_ref[...] = jnp.zeros_like(cv_ref)

    def stage(parity, do_project, do_elementwise, do_output):
        p_write, p_read = (p_even, p_odd) if parity == 0 else (p_odd, p_even)
        mix_write, mix_read = (mix_even, mix_odd) if parity == 0 else (mix_odd, mix_even)

        def project(g, i):
            p_write[g, i] = jnp.dot(xb_ref[...], w_ref[g, i], preferred_element_type=F32)

        def output():
            y_ref[...] += jnp.dot(mix_read[...], wo_ref[...], preferred_element_type=F32)

        if not do_elementwise:
            if do_project:
                for g in range(group):
                    for i in range(PROMPT_COL_BLOCKS):
                        project(g, i)
            if do_output:
                output()
            return
        pos = lax.broadcasted_iota(jnp.int32, (CHUNK, hd), 0)
        causal = (lax.broadcasted_iota(jnp.int32, (CHUNK, CHUNK), 0)
                  >= lax.broadcasted_iota(jnp.int32, (CHUNK, CHUNK), 1))
        mid = CHUNK // 2 - 1
        tails = [elementwise_unit(g, p_write, p_read, mix_write, do_project, pos, causal, mid)
                 for g in range(group)]
        if do_output:
            output()

        @pl.when(t == tiles_per_seq - 1)
        def _():
            for g in range(group):
                lanes = pl.ds(pl.multiple_of((j0 + g) * hd, hd), hd)
                cbuf_ref[:, lanes] = tails[g][SUBLANES - (CONV_WIDTH - 1):SUBLANES]
                snew_ref[j0 + g] = st_ref[j0 + g].T

    def elementwise_unit(g, p_write, p_read, mix_write, do_project, pos, causal, mid):
        j = j0 + g
        lanes = slice(g * hd, (g + 1) * hd)
        mix_lo = g * 2 * hd

        def strip(k, rows):
            lo = (k % strips_per_block) * hd
            return p_read[g, k // strips_per_block, rows, lo:lo + hd]

        cw = cw_ref[:, lanes]
        na = na_ref[:, lanes]
        nb = nb_ref[:, lanes]
        lb = _lower_bound(lbl_ref[:, lanes])

        def conv_chunk(rows, tail):
            u = strip(2, rows) * strip(0, rows)
            head = (SUBLANES, hd)
            prev1 = jnp.broadcast_to(tail[SUBLANES - 1:SUBLANES], head)
            prev2 = jnp.broadcast_to(tail[SUBLANES - 2:SUBLANES - 1], head)
            pos8 = pos[:SUBLANES]
            u1 = pltpu.roll(u, 1, axis=0)
            u1 = jnp.concatenate([jnp.where(pos8 == 0, prev1, u1[:SUBLANES]), u1[SUBLANES:]], axis=0)
            u2 = pltpu.roll(u, 2, axis=0)
            u2_head = jnp.where(pos8 == 0, prev2, jnp.where(pos8 == 1, prev1, u2[:SUBLANES]))
            u2 = jnp.concatenate([u2_head, u2[SUBLANES:]], axis=0)
            ya = _conv_apply(u, u1, u2, cw, strip(1, rows), strip(3, rows), na)
            mix_write[rows, mix_lo:mix_lo + hd] = ya.astype(BF16)
            return u[CHUNK - SUBLANES:CHUNK]

        def hgrn_chunk(rows):
            q, k, log_f = _gates(strip(4, rows), strip(5, rows), lb, hd)
            v = strip(6, rows)
            cum = _segment_cumsum(log_f, pos, CHUNK)
            c_mid = cum[mid:mid + 1]
            c_last = cum[CHUNK - 1:CHUNK]
            q_e = q * jnp.exp(cum - c_mid)
            k_e = k * jnp.exp(c_mid - cum)
            k_t = k_e * jnp.exp(c_last - c_mid)
            q_i = q_e * jnp.exp(c_mid)
            s_t = st_ref[j]
            scores = lax.dot_general(q_e.astype(BF16), k_e.astype(BF16), _NT,
                                     preferred_element_type=F32)
            probs = jnp.where(causal, scores, 0.0).astype(BF16)
            o = jnp.dot(probs, v.astype(BF16), preferred_element_type=F32)
            o = o + lax.dot_general(q_i.astype(BF16), s_t.astype(BF16), _NT,
                                    preferred_element_type=F32)
            st_ref[j] = jnp.exp(c_last) * s_t + jnp.dot(
                v.T.astype(BF16), k_t.astype(BF16), preferred_element_type=F32)
            yb = _lane_rms(o, nb) * _silu(strip(7, rows))
            mix_write[rows, mix_lo + hd:mix_lo + 2 * hd] = yb.astype(BF16)

        tail = cv_ref[j]
        for i in range(PROMPT_COL_BLOCKS):
            if do_project:
                p_write[g, i] = jnp.dot(xb_ref[...], w_ref[g, i], preferred_element_type=F32)
            for c in range(chunks_per_block):
                rows = pl.ds((i * chunks_per_block + c) * CHUNK, CHUNK)
                tail = conv_chunk(rows, tail)
                hgrn_chunk(rows)
        cv_ref[j] = tail
        return tail

    steady = (n >= 2) & (n < units)
    _run_variants([(n == 0, 0, (True, False, False)),
                   (n == 1, 1, (True, True, False)),
                   (steady & (n % 2 == 0), 0, (True, True, True)),
                   (steady & (n % 2 == 1), 1, (True, True, True)),
                   (n == units, units % 2, (False, True, True)),
                   (n == units + 1, (units + 1) % 2, (False, False, True))], stage)

    @pl.when((n >= 2) & ((n - 2) % per_tile == per_tile - 1))
    def _():
        _layer_norm_rows(y_ref, lg_ref[...], lbias_ref[...], tile)


def _sample_body(x_ref, hist_ref, s0_ref, wa_ref, wb_ref, woa_ref, wob_ref,
                 cw_ref, na_ref, lbl_ref, nb_ref, lg_ref, lbias_ref,
                 y_ref, cbuf_ref, snew_ref, wr_ref, wor_ref,
                 xb_ref, p_even, p_odd, mix_even, mix_odd, wo_b,
                 *, seqs, steps, pairs, alpha):
    jj = pl.program_id(0)
    c = pl.program_id(1)
    hd = LANES
    rows_all, d_model = x_ref.shape
    nrow = seqs * steps
    out_cols = d_model // COL_BLOCKS

    @pl.when((jj == 0) & (c == 0))
    def _():
        _cast_rows(x_ref, xb_ref, rows_all)
        _scale_rows(x_ref, y_ref, alpha, rows_all)

    @pl.when((jj >= 2) & (c == 0))
    def _():
        for ref, lo in ((woa_ref, 0), (wob_ref, hd)):
            wor_ref[lo:lo + hd, :] = ref[...].astype(BF16)
            for q in range(COL_BLOCKS):
                wo_b[q, lo:lo + hd, :] = ref[:, q * out_cols:(q + 1) * out_cols].astype(BF16)

    def stage(parity, do_project, do_elementwise, do_output):
        p_write, p_read = (p_even, p_odd) if parity == 0 else (p_odd, p_even)
        mix_write, mix_read = (mix_even, mix_odd) if parity == 0 else (mix_odd, mix_even)

        if do_project:
            wr_ref[:, 0:hd] = wa_ref[...].astype(BF16)
            wr_ref[:, hd:2 * hd] = wb_ref[...].astype(BF16)
            p_write[c] = jnp.dot(xb_ref[...], wr_ref[...], preferred_element_type=F32)
        if do_output:
            cols = pl.ds(pl.multiple_of(c * out_cols, out_cols), out_cols)
            y_ref[:, cols] += jnp.dot(mix_read[...], wo_b[c], preferred_element_type=F32)
        if not do_elementwise:
            return

        rows = pl.ds(pl.multiple_of(c * nrow, nrow), nrow)
        strip = _strip_reader(p_read)
        pos = lax.broadcasted_iota(jnp.int32, (nrow, hd), 0) % steps
        per_row = lambda a: jnp.broadcast_to(a, (seqs, steps, hd)).reshape(nrow, hd)

        hist = hist_ref[...]
        prev2 = per_row(hist[:, 0:1, :])
        prev1 = per_row(hist[:, 1:2, :])
        u = strip(2, rows) * strip(0, rows)
        u1 = jnp.where(pos == 0, prev1, pltpu.roll(u, 1, axis=0))
        u2 = jnp.where(pos == 0, prev2, jnp.where(pos == 1, prev1, pltpu.roll(u, 2, axis=0)))
        ya = _conv_apply(u, u1, u2, cw_ref[...], strip(1, rows), strip(3, rows), na_ref[...])
        mix_write[rows, 0:hd] = ya.astype(BF16)
        cbuf_ref[...] = u.reshape(seqs, steps, hd)[:, steps - (CONV_WIDTH - 1):, :]

        lb = _lower_bound(lbl_ref[...])
        q, k, g = _gates(strip(4, rows), strip(5, rows), lb, hd)
        v = strip(6, rows)
        cum = _segment_cumsum(g, pos, steps)
        c_last = per_row(cum.reshape(seqs, steps, hd)[:, steps - 1:steps, :])
        q_e = q * jnp.exp(cum)
        k_e = k * jnp.exp(-cum)
        k_t = k * jnp.exp(c_last - cum)
        decay = jnp.exp(c_last)

        ri = lax.broadcasted_iota(jnp.int32, (nrow, nrow), 0)
        ci = lax.broadcasted_iota(jnp.int32, (nrow, nrow), 1)
        same_causal = (ri // steps == ci // steps) & (ri >= ci)
        scores = lax.dot_general(q_e.astype(BF16), k_e.astype(BF16), _NT,
                                 preferred_element_type=F32)
        probs = jnp.where(same_causal, scores, 0.0).astype(BF16)
        o_intra = jnp.dot(probs, v.astype(BF16), preferred_element_type=F32)

        o_inter = []
        for s in range(seqs):
            sr = slice(s * steps, (s + 1) * steps)
            s0 = s0_ref[s]
            o_inter.append(jnp.dot(q_e[sr].astype(BF16), s0.astype(BF16),
                                   preferred_element_type=F32))
            upd = lax.dot_general(k_t[sr].astype(BF16), v[sr].astype(BF16), _TN,
                                  preferred_element_type=F32)
            decay_col = jnp.broadcast_to(decay[s * steps:s * steps + 1], (hd, hd)).T
            snew_ref[s] = decay_col * s0 + upd
        o = o_intra + jnp.concatenate(o_inter, axis=0)

        yb = _lane_rms(o, nb_ref[...]) * _silu(strip(7, rows))
        mix_write[rows, hd:2 * hd] = yb.astype(BF16)

    steady = (jj >= 2) & (jj < pairs)
    _run_variants([(jj == 0, 0, (True, False, False)),
                   (jj == 1, 1, (True, True, False)),
                   (steady & (jj % 2 == 0), 0, (True, True, True)),
                   (steady & (jj % 2 == 1), 1, (True, True, True)),
                   (jj == pairs, pairs % 2, (False, True, True)),
                   (jj == pairs + 1, (pairs + 1) % 2, (False, False, True))], stage)

    @pl.when((jj == pairs + 1) & (c == pl.num_programs(1) - 1))
    def _():
        _layer_norm_rows(y_ref, lg_ref[...], lbias_ref[...], rows_all)


def kernel(x_prompt, x_sample, state_conv, state_hgrn, w_in, conv_w, norm_a, lb_logits,
           norm_b, w_out, ln_gain, ln_bias):
    batch, seq, d_model = x_prompt.shape
    dec_batch, dec_seq, _ = x_sample.shape
    depth, _, n_proj = w_in.shape
    assert depth == 1
    heads, dk, dv = state_hgrn.shape[2:]
    w_conv = state_conv.shape[-1]
    hd = LANES
    assert dk == hd and dv == hd and w_conv == heads * hd and n_proj == STRIPS * heads * hd
    assert dec_seq == SUBLANES and dec_batch // SAMPLE_SEQS == COL_BLOCKS
    assert seq % PROMPT_TILE == 0 and PROMPT_TILE % (PROMPT_COL_BLOCKS * CHUNK) == 0
    wide = PROMPT_COLS // MXU_COLS
    assert heads % PROMPT_GROUP == 0 and heads // PROMPT_GROUP > 2
    assert d_model % (COL_BLOCKS * LANES) == 0
    alpha = (2.0 * depth) ** 0.25
    f32 = x_prompt.dtype
    last = heads - 1

    def strip_spec(arr_rows, pair_of):
        return pl.BlockSpec((arr_rows, hd), lambda *g: (0, pair_of(*g)))

    rows_s = dec_batch * dec_seq
    chunks = dec_batch // SAMPLE_SEQS
    xs = x_sample.reshape(rows_s, d_model)
    mm_s = lambda jj, c: jnp.minimum(jj, last)
    ew_s = lambda jj, c: jnp.clip(jj - 1, 0, last)
    out_s = lambda jj, c: jnp.clip(jj - 2, 0, last)
    mm_c = lambda jj, c: jnp.where(jj > last, chunks - 1, c)
    ew_c = lambda jj, c: jnp.where(jj == 0, 0, jnp.where(jj > heads, chunks - 1, c))
    resident = lambda shape: pl.BlockSpec(shape, lambda jj, c: (0, 0),
                                          pipeline_mode=pl.Buffered(1))
    state_spec = pl.BlockSpec((None, SAMPLE_SEQS, None, hd, hd),
                              lambda jj, c: (0, ew_c(jj, c), ew_s(jj, c), 0, 0))
    hist_spec = pl.BlockSpec((None, SAMPLE_SEQS, CONV_WIDTH - 1, hd),
                             lambda jj, c: (0, ew_c(jj, c), 0, ew_s(jj, c)))
    w_strip = lambda half: pl.BlockSpec(
        (d_model, hd), lambda jj, c: (0, (2 * mm_c(jj, c) + half) * heads + mm_s(jj, c)))
    wo_strip = lambda half: pl.BlockSpec(
        (hd, d_model), lambda jj, c: (half * heads + out_s(jj, c), 0))
    y_s, conv_s, hgrn_s, w_r, wo_r = pl.pallas_call(
        functools.partial(_sample_body, seqs=SAMPLE_SEQS, steps=dec_seq, pairs=heads,
                          alpha=alpha),
        grid=(heads + 2, chunks),
        in_specs=[
            resident((rows_s, d_model)),
            hist_spec,
            state_spec,
            w_strip(0), w_strip(1), wo_strip(0), wo_strip(1),
            pl.BlockSpec((None, CONV_WIDTH, hd), lambda jj, c: (0, 0, ew_s(jj, c))),
            strip_spec(1, ew_s), strip_spec(lb_logits.shape[0], ew_s), strip_spec(1, ew_s),
            pl.BlockSpec((1, d_model), lambda jj, c: (0, 0)),
            pl.BlockSpec((1, d_model), lambda jj, c: (0, 0)),
        ],
        out_specs=[
            resident((rows_s, d_model)),
            hist_spec,
            state_spec,
            pl.BlockSpec((None, None, d_model, MXU_COLS),
                         lambda jj, c: (mm_s(jj, c), mm_c(jj, c) // wide, 0, mm_c(jj, c) % wide)),
            pl.BlockSpec((None, 2 * hd, d_model), lambda jj, c: (out_s(jj, c), 0, 0)),
        ],
        out_shape=[
            jax.ShapeDtypeStruct((rows_s, d_model), f32),
            jax.ShapeDtypeStruct((depth, dec_batch, CONV_WIDTH - 1, w_conv), state_conv.dtype),
            jax.ShapeDtypeStruct((depth, dec_batch, heads, dk, dv), state_hgrn.dtype),
            jax.ShapeDtypeStruct((heads, PROMPT_COL_BLOCKS, d_model, PROMPT_COLS), BF16),
            jax.ShapeDtypeStruct((heads, 2 * hd, d_model), BF16),
        ],
        scratch_shapes=[
            pltpu.VMEM((rows_s, d_model), BF16),
            pltpu.VMEM((COL_BLOCKS, rows_s, MXU_COLS), F32),
            pltpu.VMEM((COL_BLOCKS, rows_s, MXU_COLS), F32),
            pltpu.VMEM((rows_s, 2 * hd), BF16),
            pltpu.VMEM((rows_s, 2 * hd), BF16),
            pltpu.VMEM((COL_BLOCKS, 2 * hd, d_model // COL_BLOCKS), BF16),
        ],
        compiler_params=pltpu.CompilerParams(
            dimension_semantics=("arbitrary", "arbitrary"),
            vmem_limit_bytes=VMEM_LIMIT),
        name="sample_mixer",
    )(xs, state_conv, state_hgrn, w_in[0], w_in[0], w_out[0], w_out[0],
      conv_w, norm_a, lb_logits, norm_b, ln_gain, ln_bias)

    tile = PROMPT_TILE
    tiles_per_seq = seq // tile
    group = PROMPT_GROUP
    per_tile = heads // group
    units = batch * tiles_per_seq * per_tile
    unit_p = lambda n: jnp.minimum(n, units - 1)
    unit_e = lambda n: jnp.clip(n - 1, 0, units - 1)
    unit_o = lambda n: jnp.clip(n - 2, 0, units - 1)
    group_p = lambda n: unit_p(n) % per_tile
    group_e = lambda n: unit_e(n) % per_tile
    group_o = lambda n: unit_o(n) % per_tile
    batch_e = lambda n: unit_e(n) // (per_tile * tiles_per_seq)

    def row_tile_map(unit_of):
        def index_map(n):
            row_tile = unit_of(n) // per_tile
            return (row_tile // tiles_per_seq, row_tile % tiles_per_seq, 0)
        return index_map

    def group_strip(arr_rows):
        return pl.BlockSpec((arr_rows, group * hd), lambda n: (0, group_e(n)))

    y_p, conv_p, hgrn_p = pl.pallas_call(
        functools.partial(_prompt_body, tile=tile, tiles_per_seq=tiles_per_seq, pairs=heads,
                          group=group, units=units, alpha=alpha),
        grid=(units + 2,),
        in_specs=[
            pl.BlockSpec((None, tile, d_model), row_tile_map(unit_p)),
            pl.BlockSpec((group, PROMPT_COL_BLOCKS, d_model, PROMPT_COLS),
                         lambda n: (group_p(n), 0, 0, 0)),
            pl.BlockSpec((group * 2 * hd, d_model), lambda n: (group_o(n), 0)),
            pl.BlockSpec((None, CONV_WIDTH, group * hd), lambda n: (0, 0, group_e(n))),
            group_strip(1), group_strip(lb_logits.shape[0]), group_strip(1),
            pl.BlockSpec((1, d_model), lambda n: (0, 0)),
            pl.BlockSpec((1, d_model), lambda n: (0, 0)),
        ],
        out_specs=[
            pl.BlockSpec((None, tile, d_model), row_tile_map(unit_o)),
            pl.BlockSpec((None, None, CONV_WIDTH - 1, w_conv), lambda n: (0, batch_e(n), 0, 0)),
            pl.BlockSpec((None, None, heads, hd, hd), lambda n: (0, batch_e(n), 0, 0, 0)),
        ],
        out_shape=[
            jax.ShapeDtypeStruct((batch, seq, d_model), f32),
            jax.ShapeDtypeStruct((depth, batch, CONV_WIDTH - 1, w_conv), state_conv.dtype),
            jax.ShapeDtypeStruct((depth, batch, heads, dk, dv), state_hgrn.dtype),
        ],
        scratch_shapes=[
            pltpu.VMEM((tile, d_model), BF16),
            pltpu.VMEM((group, PROMPT_COL_BLOCKS, tile, PROMPT_COLS), F32),
            pltpu.VMEM((group, PROMPT_COL_BLOCKS, tile, PROMPT_COLS), F32),
            pltpu.VMEM((tile, group * 2 * hd), BF16),
            pltpu.VMEM((tile, group * 2 * hd), BF16),
            pltpu.VMEM((heads, hd, hd), F32),
            pltpu.VMEM((heads, SUBLANES, hd), F32),
        ],
        compiler_params=pltpu.CompilerParams(
            dimension_semantics=("arbitrary",),
            vmem_limit_bytes=VMEM_LIMIT),
        name="prompt_mixer",
    )(x_prompt, w_r, wo_r.reshape(heads * 2 * hd, d_model), conv_w, norm_a, lb_logits, norm_b,
      ln_gain, ln_bias)

    return (y_p, y_s.reshape(x_sample.shape), conv_p, hgrn_p, conv_s, hgrn_s)
```

```python
import functools

import jax
import jax.numpy as jnp
from jax import lax
from jax.experimental import pallas as pl
from jax.experimental.pallas import tpu as pltpu

LANES = 128
SUBLANES = 8
MXU_COLS = 256
STRIPS = 8
COL_BLOCKS = STRIPS * LANES // MXU_COLS
PROMPT_COLS = 2 * MXU_COLS
PROMPT_COL_BLOCKS = STRIPS * LANES // PROMPT_COLS
CONV_WIDTH = 3
EPS = 1e-5
PROMPT_TILE = 512
PROMPT_GROUP = 2
CHUNK = 128
SAMPLE_SEQS = 32
VMEM_LIMIT = 60000 * 1024

F32 = jnp.float32
BF16 = jnp.bfloat16
_NT = (((1,), (1,)), ((), ()))
_TN = (((0,), (0,)), ((), ()))


def _sigmoid(x):
    return 1.0 / (1.0 + jnp.exp(-x))


def _silu(x):
    return x * _sigmoid(x)


def _lane_rms(x, gain):
    return x * lax.rsqrt(jnp.mean(x * x, axis=-1, keepdims=True) + EPS) * gain


def _lower_bound(lbl):
    e = jnp.exp(lbl - jnp.max(lbl, axis=0, keepdims=True))
    return e[0:1] / jnp.sum(e, axis=0, keepdims=True)


def _segment_cumsum(g, pos, seg):
    s = 1
    while s < min(seg, SUBLANES):
        g = g + jnp.where(pos >= s, pltpu.roll(g, s, axis=0), 0.0)
        s *= 2
    while s < seg:
        assert g.shape[0] == seg
        g = jnp.concatenate([g[:s], g[s:] + g[:-s]], axis=0)
        s *= 2
    return g


def _layer_norm_rows(y_ref, gain, bias, rows):
    def body(i, _):
        r0 = pl.multiple_of(i * CHUNK, CHUNK)
        z = y_ref[pl.ds(r0, CHUNK), :]
        mu = jnp.mean(z, axis=-1, keepdims=True)
        zc = z - mu
        var = jnp.mean(zc * zc, axis=-1, keepdims=True)
        y_ref[pl.ds(r0, CHUNK), :] = zc * lax.rsqrt(var + EPS) * gain + bias
        return 0
    lax.fori_loop(0, rows // CHUNK, body, 0)


def _cast_rows(x_ref, xb_ref, rows):
    def body(i, _):
        r0 = pl.multiple_of(i * CHUNK, CHUNK)
        xb_ref[pl.ds(r0, CHUNK), 0:x_ref.shape[-1]] = x_ref[pl.ds(r0, CHUNK), :].astype(BF16)
        return 0
    lax.fori_loop(0, rows // CHUNK, body, 0)


def _scale_rows(x_ref, y_ref, alpha, rows):
    def body(i, _):
        r0 = pl.multiple_of(i * CHUNK, CHUNK)
        y_ref[pl.ds(r0, CHUNK), :] = alpha * x_ref[pl.ds(r0, CHUNK), :]
        return 0
    lax.fori_loop(0, rows // CHUNK, body, 0)


def _gates(qb, fb, lb, dk):
    q = _silu(qb) * (dk ** -0.5)
    f = lb + (1.0 - lb) * _sigmoid(fb)
    return q, 1.0 - f, jnp.log(f)


def _strip_reader(p_ref):
    def strip(k, rows):
        lo = (k % 2) * LANES
        return p_ref[k // 2, rows, lo:lo + LANES]
    return strip


def _conv_apply(u, u1, u2, cw, b_a, z_a, na):
    conv = cw[0:1] * u2 + cw[1:2] * u1 + cw[2:3] * u
    return _lane_rms(b_a * conv, na) * _silu(z_a)


def _run_variants(variants, stage):
    for cond, parity, flags in variants:
        pl.when(cond)(functools.partial(stage, parity, *flags))


def _prompt_body(x_ref, w_ref, wo_ref, cw_ref, na_ref, lbl_ref, nb_ref, lg_ref, lbias_ref,
                 y_ref, cbuf_ref, snew_ref,
                 xb_ref, p_even, p_odd, mix_even, mix_odd, st_ref, cv_ref,
                 *, tile, tiles_per_seq, pairs, group, units, alpha):
    n = pl.program_id(0)
    hd = LANES
    chunks_per_block = tile // (PROMPT_COL_BLOCKS * CHUNK)
    strips_per_block = PROMPT_COLS // hd
    per_tile = pairs // group
    unit_e = jnp.clip(n - 1, 0, units - 1)
    j0 = (unit_e % per_tile) * group
    row_tile = unit_e // per_tile
    t = row_tile % tiles_per_seq

    @pl.when((n % per_tile == 0) & (n < units))
    def _():
        _cast_rows(x_ref, xb_ref, tile)

    @pl.when(n % per_tile == 2)
    def _():
        _scale_rows(x_ref, y_ref, alpha, tile)

    @pl.when((n >= 1) & (j0 == 0) & (t == 0))
    def _():
        st_ref[...] = jnp.zeros_like(st_ref)
        cv_ref[...] = jnp.zeros_like(cv_ref)

    def stage(parity, do_project, do_elementwise, do_output):
        p_write, p_read = (p_even, p_odd) if parity == 0 else (p_odd, p_even)
        mix_write, mix_read = (mix_even, mix_odd) if parity == 0 else (mix_odd, mix_even)

        def project(g, i):
            p_write[g, i, :, 0:PROMPT_COLS] = jnp.dot(
                    xb_ref[:, 0:x_ref.shape[-1]], w_ref[g, i], preferred_element_type=F32)

        def output():
            y_ref[...] += jnp.dot(mix_read[...], wo_ref[...], preferred_element_type=F32)

        if not do_elementwise:
            if do_project:
                for g in range(group):
                    for i in range(PROMPT_COL_BLOCKS):
                        project(g, i)
            if do_output:
                output()
            return
        pos = lax.broadcasted_iota(jnp.int32, (CHUNK, hd), 0)
        causal = (lax.broadcasted_iota(jnp.int32, (CHUNK, CHUNK), 0)
                  >= lax.broadcasted_iota(jnp.int32, (CHUNK, CHUNK), 1))
        mid = CHUNK // 2 - 1
        tails = [elementwise_unit(g, p_write, p_read, mix_write, do_project, pos, causal, mid)
                 for g in range(group)]
        if do_output:
            output()

        @pl.when(t == tiles_per_seq - 1)
        def _():
            for g in range(group):
                lanes = pl.ds(pl.multiple_of((j0 + g) * hd, hd), hd)
                cbuf_ref[:, lanes] = tails[g][SUBLANES - (CONV_WIDTH - 1):SUBLANES]
                snew_ref[j0 + g] = st_ref[j0 + g].T

    def elementwise_unit(g, p_write, p_read, mix_write, do_project, pos, causal, mid):
        j = j0 + g
        lanes = slice(g * hd, (g + 1) * hd)
        mix_lo = g * 2 * hd

        def strip(k, rows):
            lo = (k % strips_per_block) * hd
            return p_read[g, k // strips_per_block, rows, lo:lo + hd]

        cw = cw_ref[:, lanes]
        na = na_ref[:, lanes]
        nb = nb_ref[:, lanes]
        lb = _lower_bound(lbl_ref[:, lanes])

        def conv_chunk(rows, tail):
            u = strip(2, rows) * strip(0, rows)
            head = (SUBLANES, hd)
            prev1 = jnp.broadcast_to(tail[SUBLANES - 1:SUBLANES], head)
            prev2 = jnp.broadcast_to(tail[SUBLANES - 2:SUBLANES - 1], head)
            pos8 = pos[:SUBLANES]
            u1 = pltpu.roll(u, 1, axis=0)
            u1 = jnp.concatenate([jnp.where(pos8 == 0, prev1, u1[:SUBLANES]), u1[SUBLANES:]], axis=0)
            u2 = pltpu.roll(u, 2, axis=0)
            u2_head = jnp.where(pos8 == 0, prev2, jnp.where(pos8 == 1, prev1, u2[:SUBLANES]))
            u2 = jnp.concatenate([u2_head, u2[SUBLANES:]], axis=0)
            ya = _conv_apply(u, u1, u2, cw, strip(1, rows), strip(3, rows), na)
            mix_write[rows, mix_lo:mix_lo + hd] = ya.astype(BF16)
            return u[CHUNK - SUBLANES:CHUNK]

        def hgrn_chunk(rows):
            q, k, log_f = _gates(strip(4, rows), strip(5, rows), lb, hd)
            v = strip(6, rows)
            cum = _segment_cumsum(log_f, pos, CHUNK)
            c_mid = cum[mid:mid + 1]
            c_last = cum[CHUNK - 1:CHUNK]
            q_e = q * jnp.exp(cum - c_mid)
            k_e = k * jnp.exp(c_mid - cum)
            k_t = k_e * jnp.exp(c_last - c_mid)
            q_i = q_e * jnp.exp(c_mid)
            s_t = st_ref[j]
            scores = lax.dot_general(q_e.astype(BF16), k_e.astype(BF16), _NT,
                                     preferred_element_type=F32)
            probs = jnp.where(causal, scores, 0.0).astype(BF16)
            o = jnp.dot(probs, v.astype(BF16), preferred_element_type=F32)
            o = o + lax.dot_general(q_i.astype(BF16), s_t.astype(BF16), _NT,
                                    preferred_element_type=F32)
            st_ref[j] = jnp.exp(c_last) * s_t + jnp.dot(
                v.T.astype(BF16), k_t.astype(BF16), preferred_element_type=F32)
            yb = _lane_rms(o, nb) * _silu(strip(7, rows))
            mix_write[rows, mix_lo + hd:mix_lo + 2 * hd] = yb.astype(BF16)

        tail = cv_ref[j]
        for i in range(PROMPT_COL_BLOCKS):
            if do_project:
                p_write[g, i, :, 0:PROMPT_COLS] = jnp.dot(
                    xb_ref[:, 0:x_ref.shape[-1]], w_ref[g, i], preferred_element_type=F32)
            for c in range(chunks_per_block):
                rows = pl.ds((i * chunks_per_block + c) * CHUNK, CHUNK)
                tail = conv_chunk(rows, tail)
                hgrn_chunk(rows)
        cv_ref[j] = tail
        return tail

    steady = (n >= 2) & (n < units)
    _run_variants([(n == 0, 0, (True, False, False)),
                   (n == 1, 1, (True, True, False)),
                   (steady & (n % 2 == 0), 0, (True, True, True)),
                   (steady & (n % 2 == 1), 1, (True, True, True)),
                   (n == units, units % 2, (False, True, True)),
                   (n == units + 1, (units + 1) % 2, (False, False, True))], stage)

    @pl.when((n >= 2) & ((n - 2) % per_tile == per_tile - 1))
    def _():
        _layer_norm_rows(y_ref, lg_ref[...], lbias_ref[...], tile)


def _sample_body(x_ref, hist_ref, s0_ref, wa_ref, wb_ref, woa_ref, wob_ref,
                 cw_ref, na_ref, lbl_ref, nb_ref, lg_ref, lbias_ref,
                 y_ref, cbuf_ref, snew_ref, wr_ref, wor_ref,
                 xb_ref, p_even, p_odd, mix_even, mix_odd, wo_b,
                 *, seqs, steps, pairs, alpha):
    jj = pl.program_id(0)
    c = pl.program_id(1)
    hd = LANES
    rows_all, d_model = x_ref.shape
    nrow = seqs * steps
    out_cols = d_model // COL_BLOCKS

    @pl.when((jj == 0) & (c == 0))
    def _():
        _cast_rows(x_ref, xb_ref, rows_all)
        _scale_rows(x_ref, y_ref, alpha, rows_all)

    @pl.when((jj >= 2) & (c == 0))
    def _():
        for ref, lo in ((woa_ref, 0), (wob_ref, hd)):
            wor_ref[lo:lo + hd, :] = ref[...].astype(BF16)
            for q in range(COL_BLOCKS):
                wo_b[q, lo:lo + hd, :] = ref[:, q * out_cols:(q + 1) * out_cols].astype(BF16)

    def stage(parity, do_project, do_elementwise, do_output):
        p_write, p_read = (p_even, p_odd) if parity == 0 else (p_odd, p_even)
        mix_write, mix_read = (mix_even, mix_odd) if parity == 0 else (mix_odd, mix_even)

        if do_project:
            wr_ref[:, 0:hd] = wa_ref[...].astype(BF16)
            wr_ref[:, hd:2 * hd] = wb_ref[...].astype(BF16)
            p_write[c, :, 0:MXU_COLS] = jnp.dot(xb_ref[:, 0:x_ref.shape[-1]], wr_ref[...],
                                                preferred_element_type=F32)
        if do_output:
            cols = pl.ds(pl.multiple_of(c * out_cols, out_cols), out_cols)
            y_ref[:, cols] += jnp.dot(mix_read[...], wo_b[c], preferred_element_type=F32)
        if not do_elementwise:
            return

        rows = pl.ds(pl.multiple_of(c * nrow, nrow), nrow)
        strip = _strip_reader(p_read)
        pos = lax.broadcasted_iota(jnp.int32, (nrow, hd), 0) % steps
        per_row = lambda a: jnp.broadcast_to(a, (seqs, steps, hd)).reshape(nrow, hd)

        hist = hist_ref[...]
        prev2 = per_row(hist[:, 0:1, :])
        prev1 = per_row(hist[:, 1:2, :])
        u = strip(2, rows) * strip(0, rows)
        u1 = jnp.where(pos == 0, prev1, pltpu.roll(u, 1, axis=0))
        u2 = jnp.where(pos == 0, prev2, jnp.where(pos == 1, prev1, pltpu.roll(u, 2, axis=0)))
        ya = _conv_apply(u, u1, u2, cw_ref[...], strip(1, rows), strip(3, rows), na_ref[...])
        mix_write[rows, 0:hd] = ya.astype(BF16)
        cbuf_ref[...] = u.reshape(seqs, steps, hd)[:, steps - (CONV_WIDTH - 1):, :]

        lb = _lower_bound(lbl_ref[...])
        q, k, g = _gates(strip(4, rows), strip(5, rows), lb, hd)
        v = strip(6, rows)
        cum = _segment_cumsum(g, pos, steps)
        c_last = per_row(cum.reshape(seqs, steps, hd)[:, steps - 1:steps, :])
        q_e = q * jnp.exp(cum)
        k_e = k * jnp.exp(-cum)
        k_t = k * jnp.exp(c_last - cum)
        decay = jnp.exp(c_last)

        ri = lax.broadcasted_iota(jnp.int32, (nrow, nrow), 0)
        ci = lax.broadcasted_iota(jnp.int32, (nrow, nrow), 1)
        same_causal = (ri // steps == ci // steps) & (ri >= ci)
        scores = lax.dot_general(q_e.astype(BF16), k_e.astype(BF16), _NT,
                                 preferred_element_type=F32)
        probs = jnp.where(same_causal, scores, 0.0).astype(BF16)
        o_intra = jnp.dot(probs, v.astype(BF16), preferred_element_type=F32)

        o_inter = []
        for s in range(seqs):
            sr = slice(s * steps, (s + 1) * steps)
            s0 = s0_ref[s]
            o_inter.append(jnp.dot(q_e[sr].astype(BF16), s0.astype(BF16),
                                   preferred_element_type=F32))
            upd = lax.dot_general(k_t[sr].astype(BF16), v[sr].astype(BF16), _TN,
                                  preferred_element_type=F32)
            decay_col = jnp.broadcast_to(decay[s * steps:s * steps + 1], (hd, hd)).T
            snew_ref[s] = decay_col * s0 + upd
        o = o_intra + jnp.concatenate(o_inter, axis=0)

        yb = _lane_rms(o, nb_ref[...]) * _silu(strip(7, rows))
        mix_write[rows, hd:2 * hd] = yb.astype(BF16)

    steady = (jj >= 2) & (jj < pairs)
    _run_variants([(jj == 0, 0, (True, False, False)),
                   (jj == 1, 1, (True, True, False)),
                   (steady & (jj % 2 == 0), 0, (True, True, True)),
                   (steady & (jj % 2 == 1), 1, (True, True, True)),
                   (jj == pairs, pairs % 2, (False, True, True)),
                   (jj == pairs + 1, (pairs + 1) % 2, (False, False, True))], stage)

    @pl.when((jj == pairs + 1) & (c == pl.num_programs(1) - 1))
    def _():
        _layer_norm_rows(y_ref, lg_ref[...], lbias_ref[...], rows_all)


def kernel(x_prompt, x_sample, state_conv, state_hgrn, w_in, conv_w, norm_a, lb_logits,
           norm_b, w_out, ln_gain, ln_bias):
    batch, seq, d_model = x_prompt.shape
    dec_batch, dec_seq, _ = x_sample.shape
    depth, _, n_proj = w_in.shape
    assert depth == 1
    heads, dk, dv = state_hgrn.shape[2:]
    w_conv = state_conv.shape[-1]
    hd = LANES
    assert dk == hd and dv == hd and w_conv == heads * hd and n_proj == STRIPS * heads * hd
    assert dec_seq == SUBLANES and dec_batch // SAMPLE_SEQS == COL_BLOCKS
    assert seq % PROMPT_TILE == 0 and PROMPT_TILE % (PROMPT_COL_BLOCKS * CHUNK) == 0
    wide = PROMPT_COLS // MXU_COLS
    assert heads % PROMPT_GROUP == 0 and heads // PROMPT_GROUP > 2
    assert d_model % (COL_BLOCKS * LANES) == 0
    alpha = (2.0 * depth) ** 0.25
    f32 = x_prompt.dtype
    last = heads - 1

    def strip_spec(arr_rows, pair_of):
        return pl.BlockSpec((arr_rows, hd), lambda *g: (0, pair_of(*g)))

    rows_s = dec_batch * dec_seq
    chunks = dec_batch // SAMPLE_SEQS
    xs = x_sample.reshape(rows_s, d_model)
    mm_s = lambda jj, c: jnp.minimum(jj, last)
    ew_s = lambda jj, c: jnp.clip(jj - 1, 0, last)
    out_s = lambda jj, c: jnp.clip(jj - 2, 0, last)
    mm_c = lambda jj, c: jnp.where(jj > last, chunks - 1, c)
    ew_c = lambda jj, c: jnp.where(jj == 0, 0, jnp.where(jj > heads, chunks - 1, c))
    resident = lambda shape: pl.BlockSpec(shape, lambda jj, c: (0, 0),
                                          pipeline_mode=pl.Buffered(1))
    state_spec = pl.BlockSpec((None, SAMPLE_SEQS, None, hd, hd),
                              lambda jj, c: (0, ew_c(jj, c), ew_s(jj, c), 0, 0))
    hist_spec = pl.BlockSpec((None, SAMPLE_SEQS, CONV_WIDTH - 1, hd),
                             lambda jj, c: (0, ew_c(jj, c), 0, ew_s(jj, c)))
    w_strip = lambda half: pl.BlockSpec(
        (d_model, hd), lambda jj, c: (0, (2 * mm_c(jj, c) + half) * heads + mm_s(jj, c)))
    wo_strip = lambda half: pl.BlockSpec(
        (hd, d_model), lambda jj, c: (half * heads + out_s(jj, c), 0))
    y_s, conv_s, hgrn_s, w_r, wo_r = pl.pallas_call(
        functools.partial(_sample_body, seqs=SAMPLE_SEQS, steps=dec_seq, pairs=heads,
                          alpha=alpha),
        grid=(heads + 2, chunks),
        in_specs=[
            resident((rows_s, d_model)),
            hist_spec,
            state_spec,
            w_strip(0), w_strip(1), wo_strip(0), wo_strip(1),
            pl.BlockSpec((None, CONV_WIDTH, hd), lambda jj, c: (0, 0, ew_s(jj, c))),
            strip_spec(1, ew_s), strip_spec(lb_logits.shape[0], ew_s), strip_spec(1, ew_s),
            pl.BlockSpec((1, d_model), lambda jj, c: (0, 0)),
            pl.BlockSpec((1, d_model), lambda jj, c: (0, 0)),
        ],
        out_specs=[
            resident((rows_s, d_model)),
            hist_spec,
            state_spec,
            pl.BlockSpec((None, None, d_model, MXU_COLS),
                         lambda jj, c: (mm_s(jj, c), mm_c(jj, c) // wide, 0, mm_c(jj, c) % wide)),
            pl.BlockSpec((None, 2 * hd, d_model), lambda jj, c: (out_s(jj, c), 0, 0)),
        ],
        out_shape=[
            jax.ShapeDtypeStruct((rows_s, d_model), f32),
            jax.ShapeDtypeStruct((depth, dec_batch, CONV_WIDTH - 1, w_conv), state_conv.dtype),
            jax.ShapeDtypeStruct((depth, dec_batch, heads, dk, dv), state_hgrn.dtype),
            jax.ShapeDtypeStruct((heads, PROMPT_COL_BLOCKS, d_model, PROMPT_COLS), BF16),
            jax.ShapeDtypeStruct((heads, 2 * hd, d_model), BF16),
        ],
        scratch_shapes=[
            pltpu.VMEM((rows_s, d_model + LANES), BF16),
            pltpu.VMEM((COL_BLOCKS, rows_s, MXU_COLS + LANES), F32),
            pltpu.VMEM((COL_BLOCKS, rows_s, MXU_COLS + LANES), F32),
            pltpu.VMEM((rows_s, 2 * hd), BF16),
            pltpu.VMEM((rows_s, 2 * hd), BF16),
            pltpu.VMEM((COL_BLOCKS, 2 * hd, d_model // COL_BLOCKS), BF16),
        ],
        compiler_params=pltpu.CompilerParams(
            dimension_semantics=("arbitrary", "arbitrary"),
            vmem_limit_bytes=VMEM_LIMIT),
        name="sample_mixer",
    )(xs, state_conv, state_hgrn, w_in[0], w_in[0], w_out[0], w_out[0],
      conv_w, norm_a, lb_logits, norm_b, ln_gain, ln_bias)

    tile = PROMPT_TILE
    tiles_per_seq = seq // tile
    group = PROMPT_GROUP
    per_tile = heads // group
    units = batch * tiles_per_seq * per_tile
    unit_p = lambda n: jnp.minimum(n, units - 1)
    unit_e = lambda n: jnp.clip(n - 1, 0, units - 1)
    unit_o = lambda n: jnp.clip(n - 2, 0, units - 1)
    group_p = lambda n: unit_p(n) % per_tile
    group_e = lambda n: unit_e(n) % per_tile
    group_o = lambda n: unit_o(n) % per_tile
    batch_e = lambda n: unit_e(n) // (per_tile * tiles_per_seq)

    def row_tile_map(unit_of):
        def index_map(n):
            row_tile = unit_of(n) // per_tile
            return (row_tile // tiles_per_seq, row_tile % tiles_per_seq, 0)
        return index_map

    def group_strip(arr_rows):
        return pl.BlockSpec((arr_rows, group * hd), lambda n: (0, group_e(n)))

    y_p, conv_p, hgrn_p = pl.pallas_call(
        functools.partial(_prompt_body, tile=tile, tiles_per_seq=tiles_per_seq, pairs=heads,
                          group=group, units=units, alpha=alpha),
        grid=(units + 2,),
        in_specs=[
            pl.BlockSpec((None, tile, d_model), row_tile_map(unit_p)),
            pl.BlockSpec((group, PROMPT_COL_BLOCKS, d_model, PROMPT_COLS),
                         lambda n: (group_p(n), 0, 0, 0)),
            pl.BlockSpec((group * 2 * hd, d_model), lambda n: (group_o(n), 0)),
            pl.BlockSpec((None, CONV_WIDTH, group * hd), lambda n: (0, 0, group_e(n))),
            group_strip(1), group_strip(lb_logits.shape[0]), group_strip(1),
            pl.BlockSpec((1, d_model), lambda n: (0, 0)),
            pl.BlockSpec((1, d_model), lambda n: (0, 0)),
        ],
        out_specs=[
            pl.BlockSpec((None, tile, d_model), row_tile_map(unit_o)),
            pl.BlockSpec((None, None, CONV_WIDTH - 1, w_conv), lambda n: (0, batch_e(n), 0, 0)),
            pl.BlockSpec((None, None, heads, hd, hd), lambda n: (0, batch_e(n), 0, 0, 0)),
        ],
        out_shape=[
            jax.ShapeDtypeStruct((batch, seq, d_model), f32),
            jax.ShapeDtypeStruct((depth, batch, CONV_WIDTH - 1, w_conv), state_conv.dtype),
            jax.ShapeDtypeStruct((depth, batch, heads, dk, dv), state_hgrn.dtype),
        ],
        scratch_shapes=[
            pltpu.VMEM((tile, d_model + LANES), BF16),
            pltpu.VMEM((group, PROMPT_COL_BLOCKS, tile, PROMPT_COLS + LANES), F32),
            pltpu.VMEM((group, PROMPT_COL_BLOCKS, tile, PROMPT_COLS + LANES), F32),
            pltpu.VMEM((tile, group * 2 * hd), BF16),
            pltpu.VMEM((tile, group * 2 * hd), BF16),
            pltpu.VMEM((heads, hd, hd), F32),
            pltpu.VMEM((heads, SUBLANES, hd), F32),
        ],
        compiler_params=pltpu.CompilerParams(
            dimension_semantics=("arbitrary",),
            vmem_limit_bytes=VMEM_LIMIT),
        name="prompt_mixer",
    )(x_prompt, w_r, wo_r.reshape(heads * 2 * hd, d_model), conv_w, norm_a, lb_logits, norm_b,
      ln_gain, ln_bias)

    return (y_p, y_s.reshape(x_sample.shape), conv_p, hgrn_p, conv_s, hgrn_s)
```

```python
import functools

import jax
import jax.numpy as jnp
from jax import lax
from jax.experimental import pallas as pl
from jax.experimental.pallas import tpu as pltpu

LANES = 128
SUBLANES = 8
MXU_COLS = 256
STRIPS = 8
COL_BLOCKS = STRIPS * LANES // MXU_COLS
PROMPT_COLS = 2 * MXU_COLS
PROMPT_COL_BLOCKS = STRIPS * LANES // PROMPT_COLS
CONV_WIDTH = 3
EPS = 1e-5
PROMPT_TILE = 512
PROMPT_GROUP = 2
CHUNK = 128
SAMPLE_SEQS = 32
VMEM_LIMIT = 56 * 1024 * 1024

F32 = jnp.float32
BF16 = jnp.bfloat16
_NT = (((1,), (1,)), ((), ()))
_TN = (((0,), (0,)), ((), ()))


def _sigmoid(x):
    return 0.5 + 0.5 * jnp.tanh(0.5 * x)


def _silu(x):
    return x * _sigmoid(x)


def _lane_rms(x, gain):
    return x * lax.rsqrt(jnp.mean(x * x, axis=-1, keepdims=True) + EPS) * gain


def _lower_bound(lbl):
    e = jnp.exp(lbl - jnp.max(lbl, axis=0, keepdims=True))
    return e[0:1] / jnp.sum(e, axis=0, keepdims=True)


def _segment_cumsum(g, pos, seg):
    s = 1
    while s < min(seg, SUBLANES):
        g = g + jnp.where(pos >= s, pltpu.roll(g, s, axis=0), 0.0)
        s *= 2
    while s < seg:
        assert g.shape[0] == seg
        g = jnp.concatenate([g[:s], g[s:] + g[:-s]], axis=0)
        s *= 2
    return g


def _layer_norm_rows(y_ref, gain, bias, rows):
    def body(i, _):
        r0 = pl.multiple_of(i * CHUNK, CHUNK)
        z = y_ref[pl.ds(r0, CHUNK), :]
        mu = jnp.mean(z, axis=-1, keepdims=True)
        zc = z - mu
        var = jnp.mean(zc * zc, axis=-1, keepdims=True)
        y_ref[pl.ds(r0, CHUNK), :] = zc * lax.rsqrt(var + EPS) * gain + bias
        return 0
    lax.fori_loop(0, rows // CHUNK, body, 0)


def _cast_rows(x_ref, xb_ref, rows):
    def body(i, _):
        r0 = pl.multiple_of(i * CHUNK, CHUNK)
        xb_ref[pl.ds(r0, CHUNK), 0:x_ref.shape[-1]] = x_ref[pl.ds(r0, CHUNK), :].astype(BF16)
        return 0
    lax.fori_loop(0, rows // CHUNK, body, 0)


def _scale_rows(x_ref, y_ref, alpha, rows):
    def body(i, _):
        r0 = pl.multiple_of(i * CHUNK, CHUNK)
        y_ref[pl.ds(r0, CHUNK), :] = alpha * x_ref[pl.ds(r0, CHUNK), :]
        return 0
    lax.fori_loop(0, rows // CHUNK, body, 0)


def _gates(qb, fb, lb, dk):
    q = _silu(qb) * (dk ** -0.5)
    f = lb + (1.0 - lb) * _sigmoid(fb)
    return q, 1.0 - f, jnp.log(f)


def _strip_reader(p_ref):
    def strip(k, rows):
        lo = (k % 2) * LANES
        return p_ref[k // 2, rows, lo:lo + LANES]
    return strip


def _conv_apply(u, u1, u2, cw, b_a, z_a, na):
    conv = cw[0:1] * u2 + cw[1:2] * u1 + cw[2:3] * u
    return _lane_rms(b_a * conv, na) * _silu(z_a)


def _run_variants(variants, stage):
    for cond, parity, flags in variants:
        pl.when(cond)(functools.partial(stage, parity, *flags))


def _prompt_body(x_ref, w_ref, wo_ref, cw_ref, na_ref, lbl_ref, nb_ref, lg_ref, lbias_ref,
                 y_ref, cbuf_ref, snew_ref,
                 xb_ref, p_even, p_odd, mix_even, mix_odd, st_ref, cv_ref,
                 *, tile, tiles_per_seq, pairs, group, units, alpha):
    n = pl.program_id(0)
    hd = LANES
    chunks_per_block = tile // (PROMPT_COL_BLOCKS * CHUNK)
    strips_per_block = PROMPT_COLS // hd
    per_tile = pairs // group
    unit_e = jnp.clip(n - 1, 0, units - 1)
    j0 = (unit_e % per_tile) * group
    row_tile = unit_e // per_tile
    t = row_tile % tiles_per_seq

    @pl.when((n % per_tile == 0) & (n < units))
    def _():
        _cast_rows(x_ref, xb_ref, tile)

    @pl.when(n % per_tile == 2)
    def _():
        _scale_rows(x_ref, y_ref, alpha, tile)

    @pl.when((n >= 1) & (j0 == 0) & (t == 0))
    def _():
        st_ref[...] = jnp.zeros_like(st_ref)
        cv_ref[...] = jnp.zeros_like(cv_ref)

    def stage(parity, do_project, do_elementwise, do_output):
        p_write, p_read = (p_even, p_odd) if parity == 0 else (p_odd, p_even)
        mix_write, mix_read = (mix_even, mix_odd) if parity == 0 else (mix_odd, mix_even)

        def project(g, i):
            p_write[g, i, :, 0:PROMPT_COLS] = jnp.dot(
                    xb_ref[:, 0:x_ref.shape[-1]], w_ref[g, i], preferred_element_type=F32)

        def output():
            y_ref[...] += jnp.dot(mix_read[...], wo_ref[...], preferred_element_type=F32)

        if not do_elementwise:
            if do_project:
                for g in range(group):
                    for i in range(PROMPT_COL_BLOCKS):
                        project(g, i)
            if do_output:
                output()
            return
        pos = lax.broadcasted_iota(jnp.int32, (CHUNK, hd), 0)
        causal = (lax.broadcasted_iota(jnp.int32, (CHUNK, CHUNK), 0)
                  >= lax.broadcasted_iota(jnp.int32, (CHUNK, CHUNK), 1))
        mid = CHUNK // 2 - 1
        tails = [elementwise_unit(g, p_write, p_read, mix_write, do_project, pos, causal, mid)
                 for g in range(group)]
        if do_output:
            output()

        @pl.when(t == tiles_per_seq - 1)
        def _():
            for g in range(group):
                lanes = pl.ds(pl.multiple_of((j0 + g) * hd, hd), hd)
                cbuf_ref[:, lanes] = tails[g][SUBLANES - (CONV_WIDTH - 1):SUBLANES]
                snew_ref[j0 + g] = st_ref[j0 + g].T

    def elementwise_unit(g, p_write, p_read, mix_write, do_project, pos, causal, mid):
        j = j0 + g
        lanes = slice(g * hd, (g + 1) * hd)
        mix_lo = g * 2 * hd

        def strip(k, rows):
            lo = (k % strips_per_block) * hd
            return p_read[g, k // strips_per_block, rows, lo:lo + hd]

        cw = cw_ref[:, lanes]
        na = na_ref[:, lanes]
        nb = nb_ref[:, lanes]
        lb = _lower_bound(lbl_ref[:, lanes])

        def conv_chunk(rows, tail):
            u = strip(2, rows) * strip(0, rows)
            head = (SUBLANES, hd)
            prev1 = jnp.broadcast_to(tail[SUBLANES - 1:SUBLANES], head)
            prev2 = jnp.broadcast_to(tail[SUBLANES - 2:SUBLANES - 1], head)
            pos8 = pos[:SUBLANES]
            u1 = pltpu.roll(u, 1, axis=0)
            u1 = jnp.concatenate([jnp.where(pos8 == 0, prev1, u1[:SUBLANES]), u1[SUBLANES:]], axis=0)
            u2 = pltpu.roll(u, 2, axis=0)
            u2_head = jnp.where(pos8 == 0, prev2, jnp.where(pos8 == 1, prev1, u2[:SUBLANES]))
            u2 = jnp.concatenate([u2_head, u2[SUBLANES:]], axis=0)
            ya = _conv_apply(u, u1, u2, cw, strip(1, rows), strip(3, rows), na)
            mix_write[rows, mix_lo:mix_lo + hd] = ya.astype(BF16)
            return u[CHUNK - SUBLANES:CHUNK]

        def hgrn_chunk(rows):
            q, k, log_f = _gates(strip(4, rows), strip(5, rows), lb, hd)
            v = strip(6, rows)
            cum = _segment_cumsum(log_f, pos, CHUNK)
            c_mid = cum[mid:mid + 1]
            c_last = cum[CHUNK - 1:CHUNK]
            q_e = q * jnp.exp(cum - c_mid)
            k_e = k * jnp.exp(c_mid - cum)
            k_t = k_e * jnp.exp(c_last - c_mid)
            q_i = q_e * jnp.exp(c_mid)
            s_t = st_ref[j]
            scores = lax.dot_general(q_e.astype(BF16), k_e.astype(BF16), _NT,
                                     preferred_element_type=F32)
            probs = jnp.where(causal, scores, 0.0).astype(BF16)
            o = jnp.dot(probs, v.astype(BF16), preferred_element_type=F32)
            o = o + lax.dot_general(q_i.astype(BF16), s_t.astype(BF16), _NT,
                                    preferred_element_type=F32)
            st_ref[j] = jnp.exp(c_last) * s_t + jnp.dot(
                v.T.astype(BF16), k_t.astype(BF16), preferred_element_type=F32)
            yb = _lane_rms(o, nb) * _silu(strip(7, rows))
            mix_write[rows, mix_lo + hd:mix_lo + 2 * hd] = yb.astype(BF16)

        tail = cv_ref[j]
        for i in range(PROMPT_COL_BLOCKS):
            if do_project:
                p_write[g, i, :, 0:PROMPT_COLS] = jnp.dot(
                    xb_ref[:, 0:x_ref.shape[-1]], w_ref[g, i], preferred_element_type=F32)
            for c in range(chunks_per_block):
                rows = pl.ds((i * chunks_per_block + c) * CHUNK, CHUNK)
                tail = conv_chunk(rows, tail)
                hgrn_chunk(rows)
        cv_ref[j] = tail
        return tail

    steady = (n >= 2) & (n < units)
    _run_variants([(n == 0, 0, (True, False, False)),
                   (n == 1, 1, (True, True, False)),
                   (steady & (n % 2 == 0), 0, (True, True, True)),
                   (steady & (n % 2 == 1), 1, (True, True, True)),
                   (n == units, units % 2, (False, True, True)),
                   (n == units + 1, (units + 1) % 2, (False, False, True))], stage)

    @pl.when((n >= 2) & ((n - 2) % per_tile == per_tile - 1))
    def _():
        _layer_norm_rows(y_ref, lg_ref[...], lbias_ref[...], tile)


def _sample_body(x_ref, hist_ref, s0_ref, wa_ref, wb_ref, woa_ref, wob_ref,
                 cw_ref, na_ref, lbl_ref, nb_ref, lg_ref, lbias_ref,
                 y_ref, cbuf_ref, snew_ref, wr_ref, wor_ref,
                 xb_ref, p_even, p_odd, mix_even, mix_odd, wo_b,
                 *, seqs, steps, pairs, alpha):
    jj = pl.program_id(0)
    c = pl.program_id(1)
    hd = LANES
    rows_all, d_model = x_ref.shape
    nrow = seqs * steps
    out_cols = d_model // COL_BLOCKS

    @pl.when((jj == 0) & (c == 0))
    def _():
        _cast_rows(x_ref, xb_ref, rows_all)
        _scale_rows(x_ref, y_ref, alpha, rows_all)

    @pl.when((jj >= 2) & (c == 0))
    def _():
        for ref, lo in ((woa_ref, 0), (wob_ref, hd)):
            wor_ref[lo:lo + hd, :] = ref[...].astype(BF16)
            for q in range(COL_BLOCKS):
                wo_b[q, lo:lo + hd, :] = ref[:, q * out_cols:(q + 1) * out_cols].astype(BF16)

    def stage(parity, do_project, do_elementwise, do_output):
        p_write, p_read = (p_even, p_odd) if parity == 0 else (p_odd, p_even)
        mix_write, mix_read = (mix_even, mix_odd) if parity == 0 else (mix_odd, mix_even)

        if do_project:
            wr_ref[:, 0:hd] = wa_ref[...].astype(BF16)
            wr_ref[:, hd:2 * hd] = wb_ref[...].astype(BF16)
            p_write[c] = jnp.dot(xb_ref[:, 0:x_ref.shape[-1]], wr_ref[...], preferred_element_type=F32)
        if do_output:
            cols = pl.ds(pl.multiple_of(c * out_cols, out_cols), out_cols)
            y_ref[:, cols] += jnp.dot(mix_read[...], wo_b[c], preferred_element_type=F32)
        if not do_elementwise:
            return

        rows = pl.ds(pl.multiple_of(c * nrow, nrow), nrow)
        strip = _strip_reader(p_read)
        pos = lax.broadcasted_iota(jnp.int32, (nrow, hd), 0) % steps
        per_row = lambda a: jnp.broadcast_to(a, (seqs, steps, hd)).reshape(nrow, hd)

        hist = hist_ref[...]
        prev2 = per_row(hist[:, 0:1, :])
        prev1 = per_row(hist[:, 1:2, :])
        u = strip(2, rows) * strip(0, rows)
        u1 = jnp.where(pos == 0, prev1, pltpu.roll(u, 1, axis=0))
        u2 = jnp.where(pos == 0, prev2, jnp.where(pos == 1, prev1, pltpu.roll(u, 2, axis=0)))
        ya = _conv_apply(u, u1, u2, cw_ref[...], strip(1, rows), strip(3, rows), na_ref[...])
        mix_write[rows, 0:hd] = ya.astype(BF16)
        cbuf_ref[...] = u.reshape(seqs, steps, hd)[:, steps - (CONV_WIDTH - 1):, :]

        lb = _lower_bound(lbl_ref[...])
        q, k, g = _gates(strip(4, rows), strip(5, rows), lb, hd)
        v = strip(6, rows)
        cum = _segment_cumsum(g, pos, steps)
        c_last = per_row(cum.reshape(seqs, steps, hd)[:, steps - 1:steps, :])
        q_e = q * jnp.exp(cum)
        k_e = k * jnp.exp(-cum)
        k_t = k * jnp.exp(c_last - cum)
        decay = jnp.exp(c_last)

        ri = lax.broadcasted_iota(jnp.int32, (nrow, nrow), 0)
        ci = lax.broadcasted_iota(jnp.int32, (nrow, nrow), 1)
        same_causal = (ri // steps == ci // steps) & (ri >= ci)
        scores = lax.dot_general(q_e.astype(BF16), k_e.astype(BF16), _NT,
                                 preferred_element_type=F32)
        probs = jnp.where(same_causal, scores, 0.0).astype(BF16)
        o_intra = jnp.dot(probs, v.astype(BF16), preferred_element_type=F32)

        o_inter = []
        for s in range(seqs):
            sr = slice(s * steps, (s + 1) * steps)
            s0 = s0_ref[s]
            o_inter.append(jnp.dot(q_e[sr].astype(BF16), s0.astype(BF16),
                                   preferred_element_type=F32))
            upd = lax.dot_general(k_t[sr].astype(BF16), v[sr].astype(BF16), _TN,
                                  preferred_element_type=F32)
            decay_col = jnp.broadcast_to(decay[s * steps:s * steps + 1], (hd, hd)).T
            snew_ref[s] = decay_col * s0 + upd
        o = o_intra + jnp.concatenate(o_inter, axis=0)

        yb = _lane_rms(o, nb_ref[...]) * _silu(strip(7, rows))
        mix_write[rows, hd:2 * hd] = yb.astype(BF16)

    steady = (jj >= 2) & (jj < pairs)
    _run_variants([(jj == 0, 0, (True, False, False)),
                   (jj == 1, 1, (True, True, False)),
                   (steady & (jj % 2 == 0), 0, (True, True, True)),
                   (steady & (jj % 2 == 1), 1, (True, True, True)),
                   (jj == pairs, pairs % 2, (False, True, True)),
                   (jj == pairs + 1, (pairs + 1) % 2, (False, False, True))], stage)

    @pl.when((jj == pairs + 1) & (c == pl.num_programs(1) - 1))
    def _():
        _layer_norm_rows(y_ref, lg_ref[...], lbias_ref[...], rows_all)


def kernel(x_prompt, x_sample, state_conv, state_hgrn, w_in, conv_w, norm_a, lb_logits,
           norm_b, w_out, ln_gain, ln_bias):
    batch, seq, d_model = x_prompt.shape
    dec_batch, dec_seq, _ = x_sample.shape
    depth, _, n_proj = w_in.shape
    assert depth == 1
    heads, dk, dv = state_hgrn.shape[2:]
    w_conv = state_conv.shape[-1]
    hd = LANES
    assert dk == hd and dv == hd and w_conv == heads * hd and n_proj == STRIPS * heads * hd
    assert dec_seq == SUBLANES and dec_batch // SAMPLE_SEQS == COL_BLOCKS
    assert seq % PROMPT_TILE == 0 and PROMPT_TILE % (PROMPT_COL_BLOCKS * CHUNK) == 0
    wide = PROMPT_COLS // MXU_COLS
    assert heads % PROMPT_GROUP == 0 and heads // PROMPT_GROUP > 2
    assert d_model % (COL_BLOCKS * LANES) == 0
    alpha = (2.0 * depth) ** 0.25
    f32 = x_prompt.dtype
    last = heads - 1

    def strip_spec(arr_rows, pair_of):
        return pl.BlockSpec((arr_rows, hd), lambda *g: (0, pair_of(*g)))

    rows_s = dec_batch * dec_seq
    chunks = dec_batch // SAMPLE_SEQS
    xs = x_sample.reshape(rows_s, d_model)
    mm_s = lambda jj, c: jnp.minimum(jj, last)
    ew_s = lambda jj, c: jnp.clip(jj - 1, 0, last)
    out_s = lambda jj, c: jnp.clip(jj - 2, 0, last)
    mm_c = lambda jj, c: jnp.where(jj > last, chunks - 1, c)
    ew_c = lambda jj, c: jnp.where(jj == 0, 0, jnp.where(jj > heads, chunks - 1, c))
    resident = lambda shape: pl.BlockSpec(shape, lambda jj, c: (0, 0),
                                          pipeline_mode=pl.Buffered(1))
    state_spec = pl.BlockSpec((None, SAMPLE_SEQS, None, hd, hd),
                              lambda jj, c: (0, ew_c(jj, c), ew_s(jj, c), 0, 0))
    hist_spec = pl.BlockSpec((None, SAMPLE_SEQS, CONV_WIDTH - 1, hd),
                             lambda jj, c: (0, ew_c(jj, c), 0, ew_s(jj, c)))
    w_strip = lambda half: pl.BlockSpec(
        (d_model, hd), lambda jj, c: (0, (2 * mm_c(jj, c) + half) * heads + mm_s(jj, c)))
    wo_strip = lambda half: pl.BlockSpec(
        (hd, d_model), lambda jj, c: (half * heads + out_s(jj, c), 0))
    y_s, conv_s, hgrn_s, w_r, wo_r = pl.pallas_call(
        functools.partial(_sample_body, seqs=SAMPLE_SEQS, steps=dec_seq, pairs=heads,
                          alpha=alpha),
        grid=(heads + 2, chunks),
        in_specs=[
            resident((rows_s, d_model)),
            hist_spec,
            state_spec,
            w_strip(0), w_strip(1), wo_strip(0), wo_strip(1),
            pl.BlockSpec((None, CONV_WIDTH, hd), lambda jj, c: (0, 0, ew_s(jj, c))),
            strip_spec(1, ew_s), strip_spec(lb_logits.shape[0], ew_s), strip_spec(1, ew_s),
            pl.BlockSpec((1, d_model), lambda jj, c: (0, 0)),
            pl.BlockSpec((1, d_model), lambda jj, c: (0, 0)),
        ],
        out_specs=[
            resident((rows_s, d_model)),
            hist_spec,
            state_spec,
            pl.BlockSpec((None, None, d_model, MXU_COLS),
                         lambda jj, c: (mm_s(jj, c), mm_c(jj, c) // wide, 0, mm_c(jj, c) % wide)),
            pl.BlockSpec((None, 2 * hd, d_model), lambda jj, c: (out_s(jj, c), 0, 0)),
        ],
        out_shape=[
            jax.ShapeDtypeStruct((rows_s, d_model), f32),
            jax.ShapeDtypeStruct((depth, dec_batch, CONV_WIDTH - 1, w_conv), state_conv.dtype),
            jax.ShapeDtypeStruct((depth, dec_batch, heads, dk, dv), state_hgrn.dtype),
            jax.ShapeDtypeStruct((heads, PROMPT_COL_BLOCKS, d_model, PROMPT_COLS), BF16),
            jax.ShapeDtypeStruct((heads, 2 * hd, d_model), BF16),
        ],
        scratch_shapes=[
            pltpu.VMEM((rows_s, d_model + LANES), BF16),
            pltpu.VMEM((COL_BLOCKS, rows_s, MXU_COLS), F32),
            pltpu.VMEM((COL_BLOCKS, rows_s, MXU_COLS), F32),
            pltpu.VMEM((rows_s, 2 * hd), BF16),
            pltpu.VMEM((rows_s, 2 * hd), BF16),
            pltpu.VMEM((COL_BLOCKS, 2 * hd, d_model // COL_BLOCKS), BF16),
        ],
        compiler_params=pltpu.CompilerParams(
            dimension_semantics=("arbitrary", "arbitrary"),
            vmem_limit_bytes=VMEM_LIMIT),
        name="sample_mixer",
    )(xs, state_conv, state_hgrn, w_in[0], w_in[0], w_out[0], w_out[0],
      conv_w, norm_a, lb_logits, norm_b, ln_gain, ln_bias)

    tile = PROMPT_TILE
    tiles_per_seq = seq // tile
    group = PROMPT_GROUP
    per_tile = heads // group
    units = batch * tiles_per_seq * per_tile
    unit_p = lambda n: jnp.minimum(n, units - 1)
    unit_e = lambda n: jnp.clip(n - 1, 0, units - 1)
    unit_o = lambda n: jnp.clip(n - 2, 0, units - 1)
    group_p = lambda n: unit_p(n) % per_tile
    group_e = lambda n: unit_e(n) % per_tile
    group_o = lambda n: unit_o(n) % per_tile
    batch_e = lambda n: unit_e(n) // (per_tile * tiles_per_seq)

    def row_tile_map(unit_of):
        def index_map(n):
            row_tile = unit_of(n) // per_tile
            return (row_tile // tiles_per_seq, row_tile % tiles_per_seq, 0)
        return index_map

    def group_strip(arr_rows):
        return pl.BlockSpec((arr_rows, group * hd), lambda n: (0, group_e(n)))

    y_p, conv_p, hgrn_p = pl.pallas_call(
        functools.partial(_prompt_body, tile=tile, tiles_per_seq=tiles_per_seq, pairs=heads,
                          group=group, units=units, alpha=alpha),
        grid=(units + 2,),
        in_specs=[
            pl.BlockSpec((None, tile, d_model), row_tile_map(unit_p)),
            pl.BlockSpec((group, PROMPT_COL_BLOCKS, d_model, PROMPT_COLS),
                         lambda n: (group_p(n), 0, 0, 0)),
            pl.BlockSpec((group * 2 * hd, d_model), lambda n: (group_o(n), 0)),
            pl.BlockSpec((None, CONV_WIDTH, group * hd), lambda n: (0, 0, group_e(n))),
            group_strip(1), group_strip(lb_logits.shape[0]), group_strip(1),
            pl.BlockSpec((1, d_model), lambda n: (0, 0)),
            pl.BlockSpec((1, d_model), lambda n: (0, 0)),
        ],
        out_specs=[
            pl.BlockSpec((None, tile, d_model), row_tile_map(unit_o)),
            pl.BlockSpec((None, None, CONV_WIDTH - 1, w_conv), lambda n: (0, batch_e(n), 0, 0)),
            pl.BlockSpec((None, None, heads, hd, hd), lambda n: (0, batch_e(n), 0, 0, 0)),
        ],
        out_shape=[
            jax.ShapeDtypeStruct((batch, seq, d_model), f32),
            jax.ShapeDtypeStruct((depth, batch, CONV_WIDTH - 1, w_conv), state_conv.dtype),
            jax.ShapeDtypeStruct((depth, batch, heads, dk, dv), state_hgrn.dtype),
        ],
        scratch_shapes=[
            pltpu.VMEM((tile, d_model + LANES), BF16),
            pltpu.VMEM((group, PROMPT_COL_BLOCKS, tile, PROMPT_COLS + LANES), F32),
            pltpu.VMEM((group, PROMPT_COL_BLOCKS, tile, PROMPT_COLS + LANES), F32),
            pltpu.VMEM((tile, group * 2 * hd), BF16),
            pltpu.VMEM((tile, group * 2 * hd), BF16),
            pltpu.VMEM((heads, hd, hd), F32),
            pltpu.VMEM((heads, SUBLANES, hd), F32),
        ],
        compiler_params=pltpu.CompilerParams(
            dimension_semantics=("arbitrary",),
            vmem_limit_bytes=VMEM_LIMIT),
        name="prompt_mixer",
    )(x_prompt, w_r, wo_r.reshape(heads * 2 * hd, d_model), conv_w, norm_a, lb_logits, norm_b,
      ln_gain, ln_bias)

    return (y_p, y_s.reshape(x_sample.shape), conv_p, hgrn_p, conv_s, hgrn_s)
```

```python
import functools

import jax
import jax.numpy as jnp
from jax import lax
from jax.experimental import pallas as pl
from jax.experimental.pallas import tpu as pltpu

LANES = 128
SUBLANES = 8
MXU_COLS = 256
STRIPS = 8
COL_BLOCKS = STRIPS * LANES // MXU_COLS
PROMPT_COLS = 2 * MXU_COLS
PROMPT_COL_BLOCKS = STRIPS * LANES // PROMPT_COLS
CONV_WIDTH = 3
EPS = 1e-5
PROMPT_TILE = 512
PROMPT_GROUP = 2
CHUNK = 128
SAMPLE_SEQS = 32
VMEM_LIMIT = 56 * 1024 * 1024

F32 = jnp.float32
BF16 = jnp.bfloat16
_NT = (((1,), (1,)), ((), ()))
_TN = (((0,), (0,)), ((), ()))


def _sigmoid(x):
    return 0.5 + 0.5 * jnp.tanh(0.5 * x)


def _silu(x, scale=1.0):
    half = 0.5 * x
    a = half if scale == 1.0 else (0.5 * scale) * x
    return a + a * jnp.tanh(half)


def _lane_rms(x, gain):
    return x * lax.rsqrt(jnp.mean(x * x, axis=-1, keepdims=True) + EPS) * gain


def _lower_bound(lbl):
    e = jnp.exp(lbl - jnp.max(lbl, axis=0, keepdims=True))
    return e[0:1] / jnp.sum(e, axis=0, keepdims=True)


def _segment_cumsum(g, pos, seg):
    s = 1
    while s < min(seg, SUBLANES):
        g = g + jnp.where(pos >= s, pltpu.roll(g, s, axis=0), 0.0)
        s *= 2
    while s < seg:
        assert g.shape[0] == seg
        g = jnp.concatenate([g[:s], g[s:] + g[:-s]], axis=0)
        s *= 2
    return g


def _layer_norm_rows(y_ref, gain, bias, rows):
    def body(i, _):
        r0 = pl.multiple_of(i * CHUNK, CHUNK)
        z = y_ref[pl.ds(r0, CHUNK), :]
        mu = jnp.mean(z, axis=-1, keepdims=True)
        zc = z - mu
        var = jnp.mean(zc * zc, axis=-1, keepdims=True)
        y_ref[pl.ds(r0, CHUNK), :] = zc * lax.rsqrt(var + EPS) * gain + bias
        return 0
    lax.fori_loop(0, rows // CHUNK, body, 0)


def _cast_rows(x_ref, xb_ref, rows):
    def body(i, _):
        r0 = pl.multiple_of(i * CHUNK, CHUNK)
        xb_ref[pl.ds(r0, CHUNK), 0:x_ref.shape[-1]] = x_ref[pl.ds(r0, CHUNK), :].astype(BF16)
        return 0
    lax.fori_loop(0, rows // CHUNK, body, 0)


def _scale_rows(x_ref, y_ref, alpha, rows):
    def body(i, _):
        r0 = pl.multiple_of(i * CHUNK, CHUNK)
        y_ref[pl.ds(r0, CHUNK), :] = alpha * x_ref[pl.ds(r0, CHUNK), :]
        return 0
    lax.fori_loop(0, rows // CHUNK, body, 0)


def _gates(qb, fb, lb, dk):
    q = _silu(qb, dk ** -0.5)
    f = lb + (1.0 - lb) * _sigmoid(fb)
    return q, 1.0 - f, jnp.log2(f)


def _strip_reader(p_ref):
    def strip(k, rows):
        lo = (k % 2) * LANES
        return p_ref[k // 2, rows, lo:lo + LANES]
    return strip


def _conv_apply(u, u1, u2, cw, b_a, z_a, na):
    conv = cw[0:1] * u2 + cw[1:2] * u1 + cw[2:3] * u
    return _lane_rms(b_a * conv, na) * _silu(z_a)


def _run_variants(variants, stage):
    for cond, parity, flags in variants:
        pl.when(cond)(functools.partial(stage, parity, *flags))


def _prompt_body(x_ref, w_ref, wo_ref, cw_ref, na_ref, lbl_ref, nb_ref, lg_ref, lbias_ref,
                 y_ref, cbuf_ref, snew_ref,
                 xb_ref, p_even, p_odd, mix_even, mix_odd, st_ref, cv_ref,
                 *, tile, tiles_per_seq, pairs, group, units, alpha):
    n = pl.program_id(0)
    hd = LANES
    chunks_per_block = tile // (PROMPT_COL_BLOCKS * CHUNK)
    strips_per_block = PROMPT_COLS // hd
    per_tile = pairs // group
    unit_e = jnp.clip(n - 1, 0, units - 1)
    j0 = (unit_e % per_tile) * group
    row_tile = unit_e // per_tile
    t = row_tile % tiles_per_seq

    @pl.when((n % per_tile == 0) & (n < units))
    def _():
        _cast_rows(x_ref, xb_ref, tile)

    @pl.when(n % per_tile == 2)
    def _():
        _scale_rows(x_ref, y_ref, alpha, tile)

    @pl.when((n >= 1) & (j0 == 0) & (t == 0))
    def _():
        st_ref[...] = jnp.zeros_like(st_ref)
        cv_ref[...] = jnp.zeros_like(cv_ref)

    def stage(parity, do_project, do_elementwise, do_output):
        p_write, p_read = (p_even, p_odd) if parity == 0 else (p_odd, p_even)
        mix_write, mix_read = (mix_even, mix_odd) if parity == 0 else (mix_odd, mix_even)

        def project(g, i):
            p_write[g, i, :, 0:PROMPT_COLS] = jnp.dot(
                    xb_ref[:, 0:x_ref.shape[-1]], w_ref[g, i], preferred_element_type=F32)

        def output():
            y_ref[...] += jnp.dot(mix_read[...], wo_ref[...], preferred_element_type=F32)

        if not do_elementwise:
            if do_project:
                for g in range(group):
                    for i in range(PROMPT_COL_BLOCKS):
                        project(g, i)
            if do_output:
                output()
            return
        pos = lax.broadcasted_iota(jnp.int32, (CHUNK, hd), 0)
        causal = (lax.broadcasted_iota(jnp.int32, (CHUNK, CHUNK), 0)
                  >= lax.broadcasted_iota(jnp.int32, (CHUNK, CHUNK), 1))
        mid = CHUNK // 2 - 1
        tails = [elementwise_unit(g, p_write, p_read, mix_write, do_project, pos, causal, mid)
                 for g in range(group)]
        if do_output:
            output()

        @pl.when(t == tiles_per_seq - 1)
        def _():
            for g in range(group):
                lanes = pl.ds(pl.multiple_of((j0 + g) * hd, hd), hd)
                cbuf_ref[:, lanes] = tails[g][SUBLANES - (CONV_WIDTH - 1):SUBLANES]
                snew_ref[j0 + g] = st_ref[j0 + g].T

    def elementwise_unit(g, p_write, p_read, mix_write, do_project, pos, causal, mid):
        j = j0 + g
        lanes = slice(g * hd, (g + 1) * hd)
        mix_lo = g * 2 * hd

        def strip(k, rows):
            lo = (k % strips_per_block) * hd
            return p_read[g, k // strips_per_block, rows, lo:lo + hd]

        cw = cw_ref[:, lanes]
        na = na_ref[:, lanes]
        nb = nb_ref[:, lanes]
        lb = _lower_bound(lbl_ref[:, lanes])

        def conv_chunk(rows, tail):
            u = strip(2, rows) * strip(0, rows)
            head = (SUBLANES, hd)
            prev1 = jnp.broadcast_to(tail[SUBLANES - 1:SUBLANES], head)
            prev2 = jnp.broadcast_to(tail[SUBLANES - 2:SUBLANES - 1], head)
            pos8 = pos[:SUBLANES]
            u1 = pltpu.roll(u, 1, axis=0)
            u1 = jnp.concatenate([jnp.where(pos8 == 0, prev1, u1[:SUBLANES]), u1[SUBLANES:]], axis=0)
            u2 = pltpu.roll(u, 2, axis=0)
            u2_head = jnp.where(pos8 == 0, prev2, jnp.where(pos8 == 1, prev1, u2[:SUBLANES]))
            u2 = jnp.concatenate([u2_head, u2[SUBLANES:]], axis=0)
            ya = _conv_apply(u, u1, u2, cw, strip(1, rows), strip(3, rows), na)
            mix_write[rows, mix_lo:mix_lo + hd] = ya.astype(BF16)
            return u[CHUNK - SUBLANES:CHUNK]

        def hgrn_chunk(rows):
            q, k, log_f = _gates(strip(4, rows), strip(5, rows), lb, hd)
            v = strip(6, rows)
            cum = _segment_cumsum(log_f, pos, CHUNK)
            c_mid = cum[mid:mid + 1]
            c_last = cum[CHUNK - 1:CHUNK]
            q_e = q * jnp.exp2(cum - c_mid)
            k_e = k * jnp.exp2(c_mid - cum)
            k_t = k_e * jnp.exp2(c_last - c_mid)
            q_i = q_e * jnp.exp2(c_mid)
            s_t = st_ref[j]
            scores = lax.dot_general(q_e.astype(BF16), k_e.astype(BF16), _NT,
                                     preferred_element_type=F32)
            probs = jnp.where(causal, scores, 0.0).astype(BF16)
            o = jnp.dot(probs, v.astype(BF16), preferred_element_type=F32)
            o = o + lax.dot_general(q_i.astype(BF16), s_t.astype(BF16), _NT,
                                    preferred_element_type=F32)
            st_ref[j] = jnp.exp2(c_last) * s_t + jnp.dot(
                v.T.astype(BF16), k_t.astype(BF16), preferred_element_type=F32)
            yb = _lane_rms(o, nb) * _silu(strip(7, rows))
            mix_write[rows, mix_lo + hd:mix_lo + 2 * hd] = yb.astype(BF16)

        tail = cv_ref[j]
        for i in range(PROMPT_COL_BLOCKS):
            if do_project:
                p_write[g, i, :, 0:PROMPT_COLS] = jnp.dot(
                    xb_ref[:, 0:x_ref.shape[-1]], w_ref[g, i], preferred_element_type=F32)
            for c in range(chunks_per_block):
                rows = pl.ds((i * chunks_per_block + c) * CHUNK, CHUNK)
                tail = conv_chunk(rows, tail)
                hgrn_chunk(rows)
        cv_ref[j] = tail
        return tail

    steady = (n >= 2) & (n < units)
    _run_variants([(n == 0, 0, (True, False, False)),
                   (n == 1, 1, (True, True, False)),
                   (steady & (n % 2 == 0), 0, (True, True, True)),
                   (steady & (n % 2 == 1), 1, (True, True, True)),
                   (n == units, units % 2, (False, True, True)),
                   (n == units + 1, (units + 1) % 2, (False, False, True))], stage)

    @pl.when((n >= 2) & ((n - 2) % per_tile == per_tile - 1))
    def _():
        _layer_norm_rows(y_ref, lg_ref[...], lbias_ref[...], tile)


def _sample_body(x_ref, hist_ref, s0_ref, wa_ref, wb_ref, woa_ref, wob_ref,
                 cw_ref, na_ref, lbl_ref, nb_ref, lg_ref, lbias_ref,
                 y_ref, cbuf_ref, snew_ref, wr_ref, wor_ref,
                 xb_ref, p_even, p_odd, mix_even, mix_odd, wo_b,
                 *, seqs, steps, pairs, alpha):
    jj = pl.program_id(0)
    c = pl.program_id(1)
    hd = LANES
    rows_all, d_model = x_ref.shape
    nrow = seqs * steps
    out_cols = d_model // COL_BLOCKS

    @pl.when((jj == 0) & (c == 0))
    def _():
        _cast_rows(x_ref, xb_ref, rows_all)
        _scale_rows(x_ref, y_ref, alpha, rows_all)

    @pl.when((jj >= 2) & (c == 0))
    def _():
        for ref, lo in ((woa_ref, 0), (wob_ref, hd)):
            wor_ref[lo:lo + hd, :] = ref[...].astype(BF16)
            for q in range(COL_BLOCKS):
                wo_b[q, lo:lo + hd, :] = ref[:, q * out_cols:(q + 1) * out_cols].astype(BF16)

    def stage(parity, do_project, do_elementwise, do_output):
        p_write, p_read = (p_even, p_odd) if parity == 0 else (p_odd, p_even)
        mix_write, mix_read = (mix_even, mix_odd) if parity == 0 else (mix_odd, mix_even)

        if do_project:
            wr_ref[:, 0:hd] = wa_ref[...].astype(BF16)
            wr_ref[:, hd:2 * hd] = wb_ref[...].astype(BF16)
            p_write[c] = jnp.dot(xb_ref[:, 0:x_ref.shape[-1]], wr_ref[...], preferred_element_type=F32)
        if do_output:
            cols = pl.ds(pl.multiple_of(c * out_cols, out_cols), out_cols)
            y_ref[:, cols] += jnp.dot(mix_read[...], wo_b[c], preferred_element_type=F32)
        if not do_elementwise:
            return

        rows = pl.ds(pl.multiple_of(c * nrow, nrow), nrow)
        strip = _strip_reader(p_read)
        pos = lax.broadcasted_iota(jnp.int32, (nrow, hd), 0) % steps
        per_row = lambda a: jnp.broadcast_to(a, (seqs, steps, hd)).reshape(nrow, hd)

        hist = hist_ref[...]
        prev2 = per_row(hist[:, 0:1, :])
        prev1 = per_row(hist[:, 1:2, :])
        u = strip(2, rows) * strip(0, rows)
        u1 = jnp.where(pos == 0, prev1, pltpu.roll(u, 1, axis=0))
        u2 = jnp.where(pos == 0, prev2, jnp.where(pos == 1, prev1, pltpu.roll(u, 2, axis=0)))
        ya = _conv_apply(u, u1, u2, cw_ref[...], strip(1, rows), strip(3, rows), na_ref[...])
        mix_write[rows, 0:hd] = ya.astype(BF16)
        cbuf_ref[...] = u.reshape(seqs, steps, hd)[:, steps - (CONV_WIDTH - 1):, :]

        lb = _lower_bound(lbl_ref[...])
        q, k, g = _gates(strip(4, rows), strip(5, rows), lb, hd)
        v = strip(6, rows)
        cum = _segment_cumsum(g, pos, steps)
        c_last = per_row(cum.reshape(seqs, steps, hd)[:, steps - 1:steps, :])
        q_e = q * jnp.exp2(cum)
        k_e = k * jnp.exp2(-cum)
        k_t = k * jnp.exp2(c_last - cum)
        decay = jnp.exp2(c_last)

        ri = lax.broadcasted_iota(jnp.int32, (nrow, nrow), 0)
        ci = lax.broadcasted_iota(jnp.int32, (nrow, nrow), 1)
        same_causal = (ri // steps == ci // steps) & (ri >= ci)
        scores = lax.dot_general(q_e.astype(BF16), k_e.astype(BF16), _NT,
                                 preferred_element_type=F32)
        probs = jnp.where(same_causal, scores, 0.0).astype(BF16)
        o_intra = jnp.dot(probs, v.astype(BF16), preferred_element_type=F32)

        o_inter = []
        for s in range(seqs):
            sr = slice(s * steps, (s + 1) * steps)
            s0 = s0_ref[s]
            o_inter.append(jnp.dot(q_e[sr].astype(BF16), s0.astype(BF16),
                                   preferred_element_type=F32))
            upd = lax.dot_general(k_t[sr].astype(BF16), v[sr].astype(BF16), _TN,
                                  preferred_element_type=F32)
            decay_col = jnp.broadcast_to(decay[s * steps:s * steps + 1], (hd, hd)).T
            snew_ref[s] = decay_col * s0 + upd
        o = o_intra + jnp.concatenate(o_inter, axis=0)

        yb = _lane_rms(o, nb_ref[...]) * _silu(strip(7, rows))
        mix_write[rows, hd:2 * hd] = yb.astype(BF16)

    steady = (jj >= 2) & (jj < pairs)
    _run_variants([(jj == 0, 0, (True, False, False)),
                   (jj == 1, 1, (True, True, False)),
                   (steady & (jj % 2 == 0), 0, (True, True, True)),
                   (steady & (jj % 2 == 1), 1, (True, True, True)),
                   (jj == pairs, pairs % 2, (False, True, True)),
                   (jj == pairs + 1, (pairs + 1) % 2, (False, False, True))], stage)

    @pl.when((jj == pairs + 1) & (c == pl.num_programs(1) - 1))
    def _():
        _layer_norm_rows(y_ref, lg_ref[...], lbias_ref[...], rows_all)


def kernel(x_prompt, x_sample, state_conv, state_hgrn, w_in, conv_w, norm_a, lb_logits,
           norm_b, w_out, ln_gain, ln_bias):
    batch, seq, d_model = x_prompt.shape
    dec_batch, dec_seq, _ = x_sample.shape
    depth, _, n_proj = w_in.shape
    assert depth == 1
    heads, dk, dv = state_hgrn.shape[2:]
    w_conv = state_conv.shape[-1]
    hd = LANES
    assert dk == hd and dv == hd and w_conv == heads * hd and n_proj == STRIPS * heads * hd
    assert dec_seq == SUBLANES and dec_batch // SAMPLE_SEQS == COL_BLOCKS
    assert seq % PROMPT_TILE == 0 and PROMPT_TILE % (PROMPT_COL_BLOCKS * CHUNK) == 0
    wide = PROMPT_COLS // MXU_COLS
    assert heads % PROMPT_GROUP == 0 and heads // PROMPT_GROUP > 2
    assert d_model % (COL_BLOCKS * LANES) == 0
    alpha = (2.0 * depth) ** 0.25
    f32 = x_prompt.dtype
    last = heads - 1

    def strip_spec(arr_rows, pair_of):
        return pl.BlockSpec((arr_rows, hd), lambda *g: (0, pair_of(*g)))

    rows_s = dec_batch * dec_seq
    chunks = dec_batch // SAMPLE_SEQS
    xs = x_sample.reshape(rows_s, d_model)
    mm_s = lambda jj, c: jnp.minimum(jj, last)
    ew_s = lambda jj, c: jnp.clip(jj - 1, 0, last)
    out_s = lambda jj, c: jnp.clip(jj - 2, 0, last)
    mm_c = lambda jj, c: jnp.where(jj > last, chunks - 1, c)
    ew_c = lambda jj, c: jnp.where(jj == 0, 0, jnp.where(jj > heads, chunks - 1, c))
    resident = lambda shape: pl.BlockSpec(shape, lambda jj, c: (0, 0),
                                          pipeline_mode=pl.Buffered(1))
    state_spec = pl.BlockSpec((None, SAMPLE_SEQS, None, hd, hd),
                              lambda jj, c: (0, ew_c(jj, c), ew_s(jj, c), 0, 0))
    hist_spec = pl.BlockSpec((None, SAMPLE_SEQS, CONV_WIDTH - 1, hd),
                             lambda jj, c: (0, ew_c(jj, c), 0, ew_s(jj, c)))
    w_strip = lambda half: pl.BlockSpec(
        (d_model, hd), lambda jj, c: (0, (2 * mm_c(jj, c) + half) * heads + mm_s(jj, c)))
    wo_strip = lambda half: pl.BlockSpec(
        (hd, d_model), lambda jj, c: (half * heads + out_s(jj, c), 0))
    y_s, conv_s, hgrn_s, w_r, wo_r = pl.pallas_call(
        functools.partial(_sample_body, seqs=SAMPLE_SEQS, steps=dec_seq, pairs=heads,
                          alpha=alpha),
        grid=(heads + 2, chunks),
        in_specs=[
            resident((rows_s, d_model)),
            hist_spec,
            state_spec,
            w_strip(0), w_strip(1), wo_strip(0), wo_strip(1),
            pl.BlockSpec((None, CONV_WIDTH, hd), lambda jj, c: (0, 0, ew_s(jj, c))),
            strip_spec(1, ew_s), strip_spec(lb_logits.shape[0], ew_s), strip_spec(1, ew_s),
            pl.BlockSpec((1, d_model), lambda jj, c: (0, 0)),
            pl.BlockSpec((1, d_model), lambda jj, c: (0, 0)),
        ],
        out_specs=[
            resident((rows_s, d_model)),
            hist_spec,
            state_spec,
            pl.BlockSpec((None, None, d_model, MXU_COLS),
                         lambda jj, c: (mm_s(jj, c), mm_c(jj, c) // wide, 0, mm_c(jj, c) % wide)),
            pl.BlockSpec((None, 2 * hd, d_model), lambda jj, c: (out_s(jj, c), 0, 0)),
        ],
        out_shape=[
            jax.ShapeDtypeStruct((rows_s, d_model), f32),
            jax.ShapeDtypeStruct((depth, dec_batch, CONV_WIDTH - 1, w_conv), state_conv.dtype),
            jax.ShapeDtypeStruct((depth, dec_batch, heads, dk, dv), state_hgrn.dtype),
            jax.ShapeDtypeStruct((heads, PROMPT_COL_BLOCKS, d_model, PROMPT_COLS), BF16),
            jax.ShapeDtypeStruct((heads, 2 * hd, d_model), BF16),
        ],
        scratch_shapes=[
            pltpu.VMEM((rows_s, d_model + LANES), BF16),
            pltpu.VMEM((COL_BLOCKS, rows_s, MXU_COLS), F32),
            pltpu.VMEM((COL_BLOCKS, rows_s, MXU_COLS), F32),
            pltpu.VMEM((rows_s, 2 * hd), BF16),
            pltpu.VMEM((rows_s, 2 * hd), BF16),
            pltpu.VMEM((COL_BLOCKS, 2 * hd, d_model // COL_BLOCKS), BF16),
        ],
        compiler_params=pltpu.CompilerParams(
            dimension_semantics=("arbitrary", "arbitrary"),
            vmem_limit_bytes=VMEM_LIMIT),
        name="sample_mixer",
    )(xs, state_conv, state_hgrn, w_in[0], w_in[0], w_out[0], w_out[0],
      conv_w, norm_a, lb_logits, norm_b, ln_gain, ln_bias)

    tile = PROMPT_TILE
    tiles_per_seq = seq // tile
    group = PROMPT_GROUP
    per_tile = heads // group
    units = batch * tiles_per_seq * per_tile
    unit_p = lambda n: jnp.minimum(n, units - 1)
    unit_e = lambda n: jnp.clip(n - 1, 0, units - 1)
    unit_o = lambda n: jnp.clip(n - 2, 0, units - 1)
    group_p = lambda n: unit_p(n) % per_tile
    group_e = lambda n: unit_e(n) % per_tile
    group_o = lambda n: unit_o(n) % per_tile
    batch_e = lambda n: unit_e(n) // (per_tile * tiles_per_seq)

    def row_tile_map(unit_of):
        def index_map(n):
            row_tile = unit_of(n) // per_tile
            return (row_tile // tiles_per_seq, row_tile % tiles_per_seq, 0)
        return index_map

    def group_strip(arr_rows):
        return pl.BlockSpec((arr_rows, group * hd), lambda n: (0, group_e(n)))

    y_p, conv_p, hgrn_p = pl.pallas_call(
        functools.partial(_prompt_body, tile=tile, tiles_per_seq=tiles_per_seq, pairs=heads,
                          group=group, units=units, alpha=alpha),
        grid=(units + 2,),
        in_specs=[
            pl.BlockSpec((None, tile, d_model), row_tile_map(unit_p)),
            pl.BlockSpec((group, PROMPT_COL_BLOCKS, d_model, PROMPT_COLS),
                         lambda n: (group_p(n), 0, 0, 0)),
            pl.BlockSpec((group * 2 * hd, d_model), lambda n: (group_o(n), 0)),
            pl.BlockSpec((None, CONV_WIDTH, group * hd), lambda n: (0, 0, group_e(n))),
            group_strip(1), group_strip(lb_logits.shape[0]), group_strip(1),
            pl.BlockSpec((1, d_model), lambda n: (0, 0)),
            pl.BlockSpec((1, d_model), lambda n: (0, 0)),
        ],
        out_specs=[
            pl.BlockSpec((None, tile, d_model), row_tile_map(unit_o)),
            pl.BlockSpec((None, None, CONV_WIDTH - 1, w_conv), lambda n: (0, batch_e(n), 0, 0)),
            pl.BlockSpec((None, None, heads, hd, hd), lambda n: (0, batch_e(n), 0, 0, 0)),
        ],
        out_shape=[
            jax.ShapeDtypeStruct((batch, seq, d_model), f32),
            jax.ShapeDtypeStruct((depth, batch, CONV_WIDTH - 1, w_conv), state_conv.dtype),
            jax.ShapeDtypeStruct((depth, batch, heads, dk, dv), state_hgrn.dtype),
        ],
        scratch_shapes=[
            pltpu.VMEM((tile, d_model + LANES), BF16),
            pltpu.VMEM((group, PROMPT_COL_BLOCKS, tile, PROMPT_COLS + LANES), F32),
            pltpu.VMEM((group, PROMPT_COL_BLOCKS, tile, PROMPT_COLS + LANES), F32),
            pltpu.VMEM((tile, group * 2 * hd), BF16),
            pltpu.VMEM((tile, group * 2 * hd), BF16),
            pltpu.VMEM((heads, hd, hd), F32),
            pltpu.VMEM((heads, SUBLANES, hd), F32),
        ],
        compiler_params=pltpu.CompilerParams(
            dimension_semantics=("arbitrary",),
            vmem_limit_bytes=VMEM_LIMIT),
        name="prompt_mixer",
    )(x_prompt, w_r, wo_r.reshape(heads * 2 * hd, d_model), conv_w, norm_a, lb_logits, norm_b,
      ln_gain, ln_bias)

    return (y_p, y_s.reshape(x_sample.shape), conv_p, hgrn_p, conv_s, hgrn_s)
```

```python
import functools

import jax
import jax.numpy as jnp
from jax import lax
from jax.experimental import pallas as pl
from jax.experimental.pallas import tpu as pltpu

LANES = 128
SUBLANES = 8
MXU_COLS = 256
STRIPS = 8
COL_BLOCKS = STRIPS * LANES // MXU_COLS
PROMPT_COLS = 2 * MXU_COLS
PROMPT_COL_BLOCKS = STRIPS * LANES // PROMPT_COLS
CONV_WIDTH = 3
EPS = 1e-5
PROMPT_TILE = 512
PROMPT_GROUP = 2
CHUNK = 128
SAMPLE_SEQS = 32
VMEM_LIMIT = 56 * 1024 * 1024

F32 = jnp.float32
BF16 = jnp.bfloat16
_NT = (((1,), (1,)), ((), ()))
_TN = (((0,), (0,)), ((), ()))


def _silu(x, scale=1.0):
    half = 0.5 * x
    a = half if scale == 1.0 else (0.5 * scale) * x
    return a + a * jnp.tanh(half)


def _lane_rms(x, gain):
    return x * lax.rsqrt(jnp.mean(x * x, axis=-1, keepdims=True) + EPS) * gain


def _lower_bound(lbl):
    e = jnp.exp(lbl - jnp.max(lbl, axis=0, keepdims=True))
    return e[0:1] / jnp.sum(e, axis=0, keepdims=True)


def _segment_cumsum(g, pos, seg):
    s = 1
    while s < min(seg, SUBLANES):
        g = g + jnp.where(pos >= s, pltpu.roll(g, s, axis=0), 0.0)
        s *= 2
    while s < seg:
        assert g.shape[0] == seg
        g = jnp.concatenate([g[:s], g[s:] + g[:-s]], axis=0)
        s *= 2
    return g


def _layer_norm_rows(y_ref, gain, bias, rows):
    def body(i, _):
        r0 = pl.multiple_of(i * CHUNK, CHUNK)
        z = y_ref[pl.ds(r0, CHUNK), :]
        mu = jnp.mean(z, axis=-1, keepdims=True)
        zc = z - mu
        var = jnp.mean(zc * zc, axis=-1, keepdims=True)
        y_ref[pl.ds(r0, CHUNK), :] = zc * lax.rsqrt(var + EPS) * gain + bias
        return 0
    lax.fori_loop(0, rows // CHUNK, body, 0)


def _cast_rows(x_ref, xb_ref, rows):
    def body(i, _):
        r0 = pl.multiple_of(i * CHUNK, CHUNK)
        xb_ref[pl.ds(r0, CHUNK), 0:x_ref.shape[-1]] = x_ref[pl.ds(r0, CHUNK), :].astype(BF16)
        return 0
    lax.fori_loop(0, rows // CHUNK, body, 0)


def _scale_rows(x_ref, y_ref, alpha, rows):
    def body(i, _):
        r0 = pl.multiple_of(i * CHUNK, CHUNK)
        y_ref[pl.ds(r0, CHUNK), :] = alpha * x_ref[pl.ds(r0, CHUNK), :]
        return 0
    lax.fori_loop(0, rows // CHUNK, body, 0)


def _gates(qb, fb, lb, dk):
    q = _silu(qb, dk ** -0.5)
    f = (0.5 + 0.5 * lb) + (0.5 - 0.5 * lb) * jnp.tanh(0.5 * fb)
    return q, 1.0 - f, jnp.log2(f)


def _strip_reader(p_ref):
    def strip(k, rows):
        lo = (k % 2) * LANES
        return p_ref[k // 2, rows, lo:lo + LANES]
    return strip


def _conv_apply(u, u1, u2, cw, b_a, z_a, na):
    conv = cw[0:1] * u2 + cw[1:2] * u1 + cw[2:3] * u
    return _lane_rms(b_a * conv, na) * _silu(z_a)


def _run_variants(variants, stage):
    for cond, parity, flags in variants:
        pl.when(cond)(functools.partial(stage, parity, *flags))


def _prompt_body(x_ref, w_ref, wo_ref, cw_ref, na_ref, lbl_ref, nb_ref, lg_ref, lbias_ref,
                 y_ref, cbuf_ref, snew_ref,
                 xb_ref, p_even, p_odd, mix_even, mix_odd, st_ref, cv_ref,
                 *, tile, tiles_per_seq, pairs, group, units, alpha):
    n = pl.program_id(0)
    hd = LANES
    chunks_per_block = tile // (PROMPT_COL_BLOCKS * CHUNK)
    strips_per_block = PROMPT_COLS // hd
    per_tile = pairs // group
    unit_e = jnp.clip(n - 1, 0, units - 1)
    j0 = (unit_e % per_tile) * group
    row_tile = unit_e // per_tile
    t = row_tile % tiles_per_seq

    @pl.when((n % per_tile == 0) & (n < units))
    def _():
        _cast_rows(x_ref, xb_ref, tile)

    @pl.when(n % per_tile == 2)
    def _():
        _scale_rows(x_ref, y_ref, alpha, tile)

    @pl.when((n >= 1) & (j0 == 0) & (t == 0))
    def _():
        st_ref[...] = jnp.zeros_like(st_ref)
        cv_ref[...] = jnp.zeros_like(cv_ref)

    def stage(parity, do_project, do_elementwise, do_output):
        p_write, p_read = (p_even, p_odd) if parity == 0 else (p_odd, p_even)
        mix_write, mix_read = (mix_even, mix_odd) if parity == 0 else (mix_odd, mix_even)

        def project(g, i):
            p_write[g, i, :, 0:PROMPT_COLS] = jnp.dot(
                    xb_ref[:, 0:x_ref.shape[-1]], w_ref[g, i], preferred_element_type=F32)

        def output():
            y_ref[...] += jnp.dot(mix_read[...], wo_ref[...], preferred_element_type=F32)

        if not do_elementwise:
            if do_project:
                for g in range(group):
                    for i in range(PROMPT_COL_BLOCKS):
                        project(g, i)
            if do_output:
                output()
            return
        pos = lax.broadcasted_iota(jnp.int32, (CHUNK, hd), 0)
        causal = (lax.broadcasted_iota(jnp.int32, (CHUNK, CHUNK), 0)
                  >= lax.broadcasted_iota(jnp.int32, (CHUNK, CHUNK), 1))
        mid = CHUNK // 2 - 1
        tails = [elementwise_unit(g, p_write, p_read, mix_write, do_project, pos, causal, mid)
                 for g in range(group)]
        if do_output:
            output()

        @pl.when(t == tiles_per_seq - 1)
        def _():
            for g in range(group):
                lanes = pl.ds(pl.multiple_of((j0 + g) * hd, hd), hd)
                cbuf_ref[:, lanes] = tails[g][SUBLANES - (CONV_WIDTH - 1):SUBLANES]
                snew_ref[j0 + g] = st_ref[j0 + g].T

    def elementwise_unit(g, p_write, p_read, mix_write, do_project, pos, causal, mid):
        j = j0 + g
        lanes = slice(g * hd, (g + 1) * hd)
        mix_lo = g * 2 * hd

        def strip(k, rows):
            lo = (k % strips_per_block) * hd
            return p_read[g, k // strips_per_block, rows, lo:lo + hd]

        cw = cw_ref[:, lanes]
        na = na_ref[:, lanes]
        nb = nb_ref[:, lanes]
        lb = _lower_bound(lbl_ref[:, lanes])

        def conv_chunk(rows, tail):
            u = strip(2, rows) * strip(0, rows)
            head = (SUBLANES, hd)
            prev1 = jnp.broadcast_to(tail[SUBLANES - 1:SUBLANES], head)
            prev2 = jnp.broadcast_to(tail[SUBLANES - 2:SUBLANES - 1], head)
            pos8 = pos[:SUBLANES]
            u1 = pltpu.roll(u, 1, axis=0)
            u1 = jnp.concatenate([jnp.where(pos8 == 0, prev1, u1[:SUBLANES]), u1[SUBLANES:]], axis=0)
            u2 = pltpu.roll(u, 2, axis=0)
            u2_head = jnp.where(pos8 == 0, prev2, jnp.where(pos8 == 1, prev1, u2[:SUBLANES]))
            u2 = jnp.concatenate([u2_head, u2[SUBLANES:]], axis=0)
            ya = _conv_apply(u, u1, u2, cw, strip(1, rows), strip(3, rows), na)
            mix_write[rows, mix_lo:mix_lo + hd] = ya.astype(BF16)
            return u[CHUNK - SUBLANES:CHUNK]

        def hgrn_chunk(rows):
            q, k, log_f = _gates(strip(4, rows), strip(5, rows), lb, hd)
            v = strip(6, rows)
            cum = _segment_cumsum(log_f, pos, CHUNK)
            c_mid = cum[mid:mid + 1]
            c_last = cum[CHUNK - 1:CHUNK]
            q_e = q * jnp.exp2(cum - c_mid)
            k_e = k * jnp.exp2(c_mid - cum)
            k_t = k_e * jnp.exp2(c_last - c_mid)
            q_i = q_e * jnp.exp2(c_mid)
            s_t = st_ref[j]
            scores = lax.dot_general(q_e.astype(BF16), k_e.astype(BF16), _NT,
                                     preferred_element_type=F32)
            probs = jnp.where(causal, scores, 0.0).astype(BF16)
            o = jnp.dot(probs, v.astype(BF16), preferred_element_type=F32)
            o = o + lax.dot_general(q_i.astype(BF16), s_t.astype(BF16), _NT,
                                    preferred_element_type=F32)
            st_ref[j] = jnp.exp2(c_last) * s_t + jnp.dot(
                v.T.astype(BF16), k_t.astype(BF16), preferred_element_type=F32)
            yb = _lane_rms(o, nb) * _silu(strip(7, rows))
            mix_write[rows, mix_lo + hd:mix_lo + 2 * hd] = yb.astype(BF16)

        tail = cv_ref[j]
        for i in range(PROMPT_COL_BLOCKS):
            if do_project:
                p_write[g, i, :, 0:PROMPT_COLS] = jnp.dot(
                    xb_ref[:, 0:x_ref.shape[-1]], w_ref[g, i], preferred_element_type=F32)
            for c in range(chunks_per_block):
                rows = pl.ds((i * chunks_per_block + c) * CHUNK, CHUNK)
                tail = conv_chunk(rows, tail)
                hgrn_chunk(rows)
        cv_ref[j] = tail
        return tail

    steady = (n >= 2) & (n < units)
    _run_variants([(n == 0, 0, (True, False, False)),
                   (n == 1, 1, (True, True, False)),
                   (steady & (n % 2 == 0), 0, (True, True, True)),
                   (steady & (n % 2 == 1), 1, (True, True, True)),
                   (n == units, units % 2, (False, True, True)),
                   (n == units + 1, (units + 1) % 2, (False, False, True))], stage)

    @pl.when((n >= 2) & ((n - 2) % per_tile == per_tile - 1))
    def _():
        _layer_norm_rows(y_ref, lg_ref[...], lbias_ref[...], tile)


def _sample_body(x_ref, hist_ref, s0_ref, wa_ref, wb_ref, woa_ref, wob_ref,
                 cw_ref, na_ref, lbl_ref, nb_ref, lg_ref, lbias_ref,
                 y_ref, cbuf_ref, snew_ref, wr_ref, wor_ref,
                 xb_ref, p_even, p_odd, mix_even, mix_odd, wo_b,
                 *, seqs, steps, pairs, alpha):
    jj = pl.program_id(0)
    c = pl.program_id(1)
    hd = LANES
    rows_all, d_model = x_ref.shape
    nrow = seqs * steps
    out_cols = d_model // COL_BLOCKS

    @pl.when((jj == 0) & (c == 0))
    def _():
        _cast_rows(x_ref, xb_ref, rows_all)
        _scale_rows(x_ref, y_ref, alpha, rows_all)

    @pl.when((jj >= 2) & (c == 0))
    def _():
        for ref, lo in ((woa_ref, 0), (wob_ref, hd)):
            wor_ref[lo:lo + hd, :] = ref[...].astype(BF16)
            for q in range(COL_BLOCKS):
                wo_b[q, lo:lo + hd, :] = ref[:, q * out_cols:(q + 1) * out_cols].astype(BF16)

    def stage(parity, do_project, do_elementwise, do_output):
        p_write, p_read = (p_even, p_odd) if parity == 0 else (p_odd, p_even)
        mix_write, mix_read = (mix_even, mix_odd) if parity == 0 else (mix_odd, mix_even)

        if do_project:
            wr_ref[:, 0:hd] = wa_ref[...].astype(BF16)
            wr_ref[:, hd:2 * hd] = wb_ref[...].astype(BF16)
            p_write[c] = jnp.dot(xb_ref[:, 0:x_ref.shape[-1]], wr_ref[...], preferred_element_type=F32)
        if do_output:
            cols = pl.ds(pl.multiple_of(c * out_cols, out_cols), out_cols)
            y_ref[:, cols] += jnp.dot(mix_read[...], wo_b[c], preferred_element_type=F32)
        if not do_elementwise:
            return

        rows = pl.ds(pl.multiple_of(c * nrow, nrow), nrow)
        strip = _strip_reader(p_read)
        pos = lax.broadcasted_iota(jnp.int32, (nrow, hd), 0) % steps
        per_row = lambda a: jnp.broadcast_to(a, (seqs, steps, hd)).reshape(nrow, hd)

        hist = hist_ref[...]
        prev2 = per_row(hist[:, 0:1, :])
        prev1 = per_row(hist[:, 1:2, :])
        u = strip(2, rows) * strip(0, rows)
        u1 = jnp.where(pos == 0, prev1, pltpu.roll(u, 1, axis=0))
        u2 = jnp.where(pos == 0, prev2, jnp.where(pos == 1, prev1, pltpu.roll(u, 2, axis=0)))
        ya = _conv_apply(u, u1, u2, cw_ref[...], strip(1, rows), strip(3, rows), na_ref[...])
        mix_write[rows, 0:hd] = ya.astype(BF16)
        cbuf_ref[...] = u.reshape(seqs, steps, hd)[:, steps - (CONV_WIDTH - 1):, :]

        lb = _lower_bound(lbl_ref[...])
        q, k, g = _gates(strip(4, rows), strip(5, rows), lb, hd)
        v = strip(6, rows)
        cum = _segment_cumsum(g, pos, steps)
        c_last = per_row(cum.reshape(seqs, steps, hd)[:, steps - 1:steps, :])
        q_e = q * jnp.exp2(cum)
        k_e = k * jnp.exp2(-cum)
        k_t = k * jnp.exp2(c_last - cum)
        decay = jnp.exp2(c_last)

        ri = lax.broadcasted_iota(jnp.int32, (nrow, nrow), 0)
        ci = lax.broadcasted_iota(jnp.int32, (nrow, nrow), 1)
        same_causal = (ri // steps == ci // steps) & (ri >= ci)
        scores = lax.dot_general(q_e.astype(BF16), k_e.astype(BF16), _NT,
                                 preferred_element_type=F32)
        probs = jnp.where(same_causal, scores, 0.0).astype(BF16)
        o_intra = jnp.dot(probs, v.astype(BF16), preferred_element_type=F32)

        o_inter = []
        for s in range(seqs):
            sr = slice(s * steps, (s + 1) * steps)
            s0 = s0_ref[s]
            o_inter.append(jnp.dot(q_e[sr].astype(BF16), s0.astype(BF16),
                                   preferred_element_type=F32))
            upd = lax.dot_general(k_t[sr].astype(BF16), v[sr].astype(BF16), _TN,
                                  preferred_element_type=F32)
            decay_col = jnp.broadcast_to(decay[s * steps:s * steps + 1], (hd, hd)).T
            snew_ref[s] = decay_col * s0 + upd
        o = o_intra + jnp.concatenate(o_inter, axis=0)

        yb = _lane_rms(o, nb_ref[...]) * _silu(strip(7, rows))
        mix_write[rows, hd:2 * hd] = yb.astype(BF16)

    steady = (jj >= 2) & (jj < pairs)
    _run_variants([(jj == 0, 0, (True, False, False)),
                   (jj == 1, 1, (True, True, False)),
                   (steady & (jj % 2 == 0), 0, (True, True, True)),
                   (steady & (jj % 2 == 1), 1, (True, True, True)),
                   (jj == pairs, pairs % 2, (False, True, True)),
                   (jj == pairs + 1, (pairs + 1) % 2, (False, False, True))], stage)

    @pl.when((jj == pairs + 1) & (c == pl.num_programs(1) - 1))
    def _():
        _layer_norm_rows(y_ref, lg_ref[...], lbias_ref[...], rows_all)


def kernel(x_prompt, x_sample, state_conv, state_hgrn, w_in, conv_w, norm_a, lb_logits,
           norm_b, w_out, ln_gain, ln_bias):
    batch, seq, d_model = x_prompt.shape
    dec_batch, dec_seq, _ = x_sample.shape
    depth, _, n_proj = w_in.shape
    assert depth == 1
    heads, dk, dv = state_hgrn.shape[2:]
    w_conv = state_conv.shape[-1]
    hd = LANES
    assert dk == hd and dv == hd and w_conv == heads * hd and n_proj == STRIPS * heads * hd
    assert dec_seq == SUBLANES and dec_batch // SAMPLE_SEQS == COL_BLOCKS
    assert seq % PROMPT_TILE == 0 and PROMPT_TILE % (PROMPT_COL_BLOCKS * CHUNK) == 0
    wide = PROMPT_COLS // MXU_COLS
    assert heads % PROMPT_GROUP == 0 and heads // PROMPT_GROUP > 2
    assert d_model % (COL_BLOCKS * LANES) == 0
    alpha = (2.0 * depth) ** 0.25
    f32 = x_prompt.dtype
    last = heads - 1

    def strip_spec(arr_rows, pair_of):
        return pl.BlockSpec((arr_rows, hd), lambda *g: (0, pair_of(*g)))

    rows_s = dec_batch * dec_seq
    chunks = dec_batch // SAMPLE_SEQS
    xs = x_sample.reshape(rows_s, d_model)
    mm_s = lambda jj, c: jnp.minimum(jj, last)
    ew_s = lambda jj, c: jnp.clip(jj - 1, 0, last)
    out_s = lambda jj, c: jnp.clip(jj - 2, 0, last)
    mm_c = lambda jj, c: jnp.where(jj > last, chunks - 1, c)
    ew_c = lambda jj, c: jnp.where(jj == 0, 0, jnp.where(jj > heads, chunks - 1, c))
    resident = lambda shape: pl.BlockSpec(shape, lambda jj, c: (0, 0),
                                          pipeline_mode=pl.Buffered(1))
    state_spec = pl.BlockSpec((None, SAMPLE_SEQS, None, hd, hd),
                              lambda jj, c: (0, ew_c(jj, c), ew_s(jj, c), 0, 0))
    hist_spec = pl.BlockSpec((None, SAMPLE_SEQS, CONV_WIDTH - 1, hd),
                             lambda jj, c: (0, ew_c(jj, c), 0, ew_s(jj, c)))
    w_strip = lambda half: pl.BlockSpec(
        (d_model, hd), lambda jj, c: (0, (2 * mm_c(jj, c) + half) * heads + mm_s(jj, c)))
    wo_strip = lambda half: pl.BlockSpec(
        (hd, d_model), lambda jj, c: (half * heads + out_s(jj, c), 0))
    y_s, conv_s, hgrn_s, w_r, wo_r = pl.pallas_call(
        functools.partial(_sample_body, seqs=SAMPLE_SEQS, steps=dec_seq, pairs=heads,
                          alpha=alpha),
        grid=(heads + 2, chunks),
        in_specs=[
            resident((rows_s, d_model)),
            hist_spec,
            state_spec,
            w_strip(0), w_strip(1), wo_strip(0), wo_strip(1),
            pl.BlockSpec((None, CONV_WIDTH, hd), lambda jj, c: (0, 0, ew_s(jj, c))),
            strip_spec(1, ew_s), strip_spec(lb_logits.shape[0], ew_s), strip_spec(1, ew_s),
            pl.BlockSpec((1, d_model), lambda jj, c: (0, 0)),
            pl.BlockSpec((1, d_model), lambda jj, c: (0, 0)),
        ],
        out_specs=[
            resident((rows_s, d_model)),
            hist_spec,
            state_spec,
            pl.BlockSpec((None, None, d_model, MXU_COLS),
                         lambda jj, c: (mm_s(jj, c), mm_c(jj, c) // wide, 0, mm_c(jj, c) % wide)),
            pl.BlockSpec((None, 2 * hd, d_model), lambda jj, c: (out_s(jj, c), 0, 0)),
        ],
        out_shape=[
            jax.ShapeDtypeStruct((rows_s, d_model), f32),
            jax.ShapeDtypeStruct((depth, dec_batch, CONV_WIDTH - 1, w_conv), state_conv.dtype),
            jax.ShapeDtypeStruct((depth, dec_batch, heads, dk, dv), state_hgrn.dtype),
            jax.ShapeDtypeStruct((heads, PROMPT_COL_BLOCKS, d_model, PROMPT_COLS), BF16),
            jax.ShapeDtypeStruct((heads, 2 * hd, d_model), BF16),
        ],
        scratch_shapes=[
            pltpu.VMEM((rows_s, d_model + LANES), BF16),
            pltpu.VMEM((COL_BLOCKS, rows_s, MXU_COLS), F32),
            pltpu.VMEM((COL_BLOCKS, rows_s, MXU_COLS), F32),
            pltpu.VMEM((rows_s, 2 * hd), BF16),
            pltpu.VMEM((rows_s, 2 * hd), BF16),
            pltpu.VMEM((COL_BLOCKS, 2 * hd, d_model // COL_BLOCKS), BF16),
        ],
        compiler_params=pltpu.CompilerParams(
            dimension_semantics=("arbitrary", "arbitrary"),
            vmem_limit_bytes=VMEM_LIMIT),
        name="sample_mixer",
    )(xs, state_conv, state_hgrn, w_in[0], w_in[0], w_out[0], w_out[0],
      conv_w, norm_a, lb_logits, norm_b, ln_gain, ln_bias)

    tile = PROMPT_TILE
    tiles_per_seq = seq // tile
    group = PROMPT_GROUP
    per_tile = heads // group
    units = batch * tiles_per_seq * per_tile
    unit_p = lambda n: jnp.minimum(n, units - 1)
    unit_e = lambda n: jnp.clip(n - 1, 0, units - 1)
    unit_o = lambda n: jnp.clip(n - 2, 0, units - 1)
    group_p = lambda n: unit_p(n) % per_tile
    group_e = lambda n: unit_e(n) % per_tile
    group_o = lambda n: unit_o(n) % per_tile
    batch_e = lambda n: unit_e(n) // (per_tile * tiles_per_seq)

    def row_tile_map(unit_of):
        def index_map(n):
            row_tile = unit_of(n) // per_tile
            return (row_tile // tiles_per_seq, row_tile % tiles_per_seq, 0)
        return index_map

    def group_strip(arr_rows):
        return pl.BlockSpec((arr_rows, group * hd), lambda n: (0, group_e(n)))

    y_p, conv_p, hgrn_p = pl.pallas_call(
        functools.partial(_prompt_body, tile=tile, tiles_per_seq=tiles_per_seq, pairs=heads,
                          group=group, units=units, alpha=alpha),
        grid=(units + 2,),
        in_specs=[
            pl.BlockSpec((None, tile, d_model), row_tile_map(unit_p)),
            pl.BlockSpec((group, PROMPT_COL_BLOCKS, d_model, PROMPT_COLS),
                         lambda n: (group_p(n), 0, 0, 0)),
            pl.BlockSpec((group * 2 * hd, d_model), lambda n: (group_o(n), 0)),
            pl.BlockSpec((None, CONV_WIDTH, group * hd), lambda n: (0, 0, group_e(n))),
            group_strip(1), group_strip(lb_logits.shape[0]), group_strip(1),
            pl.BlockSpec((1, d_model), lambda n: (0, 0)),
            pl.BlockSpec((1, d_model), lambda n: (0, 0)),
        ],
        out_specs=[
            pl.BlockSpec((None, tile, d_model), row_tile_map(unit_o)),
            pl.BlockSpec((None, None, CONV_WIDTH - 1, w_conv), lambda n: (0, batch_e(n), 0, 0)),
            pl.BlockSpec((None, None, heads, hd, hd), lambda n: (0, batch_e(n), 0, 0, 0)),
        ],
        out_shape=[
            jax.ShapeDtypeStruct((batch, seq, d_model), f32),
            jax.ShapeDtypeStruct((depth, batch, CONV_WIDTH - 1, w_conv), state_conv.dtype),
            jax.ShapeDtypeStruct((depth, batch, heads, dk, dv), state_hgrn.dtype),
        ],
        scratch_shapes=[
            pltpu.VMEM((tile, d_model + LANES), BF16),
            pltpu.VMEM((group, PROMPT_COL_BLOCKS, tile, PROMPT_COLS + LANES), F32),
            pltpu.VMEM((group, PROMPT_COL_BLOCKS, tile, PROMPT_COLS + LANES), F32),
            pltpu.VMEM((tile, group * 2 * hd), BF16),
            pltpu.VMEM((tile, group * 2 * hd), BF16),
            pltpu.VMEM((heads, hd, hd), F32),
            pltpu.VMEM((heads, SUBLANES, hd), F32),
        ],
        compiler_params=pltpu.CompilerParams(
            dimension_semantics=("arbitrary",),
            vmem_limit_bytes=VMEM_LIMIT),
        name="prompt_mixer",
    )(x_prompt, w_r, wo_r.reshape(heads * 2 * hd, d_model), conv_w, norm_a, lb_logits, norm_b,
      ln_gain, ln_bias)

    return (y_p, y_s.reshape(x_sample.shape), conv_p, hgrn_p, conv_s, hgrn_s)
```

```python
import functools

import jax
import jax.numpy as jnp
from jax import lax
from jax.experimental import pallas as pl
from jax.experimental.pallas import tpu as pltpu

LANES = 128
SUBLANES = 8
MXU_COLS = 256
STRIPS = 8
COL_BLOCKS = STRIPS * LANES // MXU_COLS
PROMPT_COLS = 2 * MXU_COLS
PROMPT_COL_BLOCKS = STRIPS * LANES // PROMPT_COLS
CONV_WIDTH = 3
EPS = 1e-5
PROMPT_TILE = 512
PROMPT_GROUP = 2
CHUNK = 128
SAMPLE_SEQS = 32
VMEM_LIMIT = 56 * 1024 * 1024

F32 = jnp.float32
BF16 = jnp.bfloat16
_NT = (((1,), (1,)), ((), ()))
_TN = (((0,), (0,)), ((), ()))


def _silu(x, scale=1.0):
    half = 0.5 * x
    a = half if scale == 1.0 else (0.5 * scale) * x
    return a + a * jnp.tanh(half)


def _lane_rms(x, gain):
    return x * lax.rsqrt(jnp.mean(x * x, axis=-1, keepdims=True) + EPS) * gain


def _lower_bound(lbl):
    e = jnp.exp(lbl - jnp.max(lbl, axis=0, keepdims=True))
    return e[0:1] / jnp.sum(e, axis=0, keepdims=True)


def _segment_cumsum(g, pos, seg):
    s = 1
    while s < min(seg, SUBLANES):
        g = g + jnp.where(pos >= s, pltpu.roll(g, s, axis=0), 0.0)
        s *= 2
    while s < seg:
        assert g.shape[0] == seg
        g = jnp.concatenate([g[:s], g[s:] + g[:-s]], axis=0)
        s *= 2
    return g


def _running_product(p, pos, reverse=False):
    n = p.shape[0]
    s = 1
    while s < n:
        if s < SUBLANES:
            if reverse:
                p = p * jnp.where(pos < n - s, pltpu.roll(p, n - s, axis=0), 1.0)
            else:
                p = p * jnp.where(pos >= s, pltpu.roll(p, s, axis=0), 1.0)
        elif reverse:
            p = jnp.concatenate([p[:n - s] * p[s:], p[n - s:]], axis=0)
        else:
            p = jnp.concatenate([p[:s], p[s:] * p[:n - s]], axis=0)
        s *= 2
    return p


def _layer_norm_rows(y_ref, gain, bias, rows):
    def body(i, _):
        r0 = pl.multiple_of(i * CHUNK, CHUNK)
        z = y_ref[pl.ds(r0, CHUNK), :]
        mu = jnp.mean(z, axis=-1, keepdims=True)
        zc = z - mu
        var = jnp.mean(zc * zc, axis=-1, keepdims=True)
        y_ref[pl.ds(r0, CHUNK), :] = zc * lax.rsqrt(var + EPS) * gain + bias
        return 0
    lax.fori_loop(0, rows // CHUNK, body, 0)


def _cast_rows(x_ref, xb_ref, rows):
    def body(i, _):
        r0 = pl.multiple_of(i * CHUNK, CHUNK)
        xb_ref[pl.ds(r0, CHUNK), 0:x_ref.shape[-1]] = x_ref[pl.ds(r0, CHUNK), :].astype(BF16)
        return 0
    lax.fori_loop(0, rows // CHUNK, body, 0)


def _scale_rows(x_ref, y_ref, alpha, rows):
    def body(i, _):
        r0 = pl.multiple_of(i * CHUNK, CHUNK)
        y_ref[pl.ds(r0, CHUNK), :] = alpha * x_ref[pl.ds(r0, CHUNK), :]
        return 0
    lax.fori_loop(0, rows // CHUNK, body, 0)


def _gates(qb, fb, lb, dk):
    q = _silu(qb, dk ** -0.5)
    f = (0.5 + 0.5 * lb) + (0.5 - 0.5 * lb) * jnp.tanh(0.5 * fb)
    return q, 1.0 - f, jnp.log2(f)


def _strip_reader(p_ref):
    def strip(k, rows):
        lo = (k % 2) * LANES
        return p_ref[k // 2, rows, lo:lo + LANES]
    return strip


def _conv_apply(u, u1, u2, cw, b_a, z_a, na):
    conv = cw[0:1] * u2 + cw[1:2] * u1 + cw[2:3] * u
    return _lane_rms(b_a * conv, na) * _silu(z_a)


def _run_variants(variants, stage):
    for cond, parity, flags in variants:
        pl.when(cond)(functools.partial(stage, parity, *flags))


def _prompt_body(x_ref, w_ref, wo_ref, cw_ref, na_ref, lbl_ref, nb_ref, lg_ref, lbias_ref,
                 y_ref, cbuf_ref, snew_ref,
                 xb_ref, p_even, p_odd, mix_even, mix_odd, st_ref, cv_ref,
                 *, tile, tiles_per_seq, pairs, group, units, alpha):
    n = pl.program_id(0)
    hd = LANES
    chunks_per_block = tile // (PROMPT_COL_BLOCKS * CHUNK)
    strips_per_block = PROMPT_COLS // hd
    per_tile = pairs // group
    unit_e = jnp.clip(n - 1, 0, units - 1)
    j0 = (unit_e % per_tile) * group
    row_tile = unit_e // per_tile
    t = row_tile % tiles_per_seq

    @pl.when((n % per_tile == 0) & (n < units))
    def _():
        _cast_rows(x_ref, xb_ref, tile)

    @pl.when(n % per_tile == 2)
    def _():
        _scale_rows(x_ref, y_ref, alpha, tile)

    @pl.when((n >= 1) & (j0 == 0) & (t == 0))
    def _():
        st_ref[...] = jnp.zeros_like(st_ref)
        cv_ref[...] = jnp.zeros_like(cv_ref)

    def stage(parity, do_project, do_elementwise, do_output):
        p_write, p_read = (p_even, p_odd) if parity == 0 else (p_odd, p_even)
        mix_write, mix_read = (mix_even, mix_odd) if parity == 0 else (mix_odd, mix_even)

        def project(g, i):
            p_write[g, i, :, 0:PROMPT_COLS] = jnp.dot(
                    xb_ref[:, 0:x_ref.shape[-1]], w_ref[g, i], preferred_element_type=F32)

        def output():
            y_ref[...] += jnp.dot(mix_read[...], wo_ref[...], preferred_element_type=F32)

        if not do_elementwise:
            if do_project:
                for g in range(group):
                    for i in range(PROMPT_COL_BLOCKS):
                        project(g, i)
            if do_output:
                output()
            return
        pos = lax.broadcasted_iota(jnp.int32, (CHUNK, hd), 0)
        causal = (lax.broadcasted_iota(jnp.int32, (CHUNK, CHUNK), 0)
                  >= lax.broadcasted_iota(jnp.int32, (CHUNK, CHUNK), 1))
        mid = CHUNK // 2 - 1
        tails = [elementwise_unit(g, p_write, p_read, mix_write, do_project, pos, causal, mid)
                 for g in range(group)]
        if do_output:
            output()

        @pl.when(t == tiles_per_seq - 1)
        def _():
            for g in range(group):
                lanes = pl.ds(pl.multiple_of((j0 + g) * hd, hd), hd)
                cbuf_ref[:, lanes] = tails[g][SUBLANES - (CONV_WIDTH - 1):SUBLANES]
                snew_ref[j0 + g] = st_ref[j0 + g].T

    def elementwise_unit(g, p_write, p_read, mix_write, do_project, pos, causal, mid):
        j = j0 + g
        lanes = slice(g * hd, (g + 1) * hd)
        mix_lo = g * 2 * hd

        def strip(k, rows):
            lo = (k % strips_per_block) * hd
            return p_read[g, k // strips_per_block, rows, lo:lo + hd]

        cw = cw_ref[:, lanes]
        na = na_ref[:, lanes]
        nb = nb_ref[:, lanes]
        lb = _lower_bound(lbl_ref[:, lanes])

        def conv_chunk(rows, tail):
            u = strip(2, rows) * strip(0, rows)
            head = (SUBLANES, hd)
            prev1 = jnp.broadcast_to(tail[SUBLANES - 1:SUBLANES], head)
            prev2 = jnp.broadcast_to(tail[SUBLANES - 2:SUBLANES - 1], head)
            pos8 = pos[:SUBLANES]
            u1 = pltpu.roll(u, 1, axis=0)
            u1 = jnp.concatenate([jnp.where(pos8 == 0, prev1, u1[:SUBLANES]), u1[SUBLANES:]], axis=0)
            u2 = pltpu.roll(u, 2, axis=0)
            u2_head = jnp.where(pos8 == 0, prev2, jnp.where(pos8 == 1, prev1, u2[:SUBLANES]))
            u2 = jnp.concatenate([u2_head, u2[SUBLANES:]], axis=0)
            ya = _conv_apply(u, u1, u2, cw, strip(1, rows), strip(3, rows), na)
            mix_write[rows, mix_lo:mix_lo + hd] = ya.astype(BF16)
            return u[CHUNK - SUBLANES:CHUNK]

        def hgrn_chunk(rows):
            q = _silu(strip(4, rows), hd ** -0.5)
            f = (0.5 + 0.5 * lb) + (0.5 - 0.5 * lb) * jnp.tanh(0.5 * strip(5, rows))
            k = 1.0 - f
            v = strip(6, rows)
            half = CHUNK // 2
            pos_h = pos[:half]
            f_lo, f_hi = f[:half], f[half:]
            nxt = jnp.where(pos_h == half - 1, 1.0, pltpu.roll(f_lo, half - 1, axis=0))
            below = _running_product(nxt, pos_h, reverse=True)
            above = _running_product(f_hi, pos_h)
            dec_mid = f_lo[0:1] * below[0:1]
            dec_hi = above[half - 1:half]
            q_e = jnp.concatenate([q[:half] / below, q[half:] * above], axis=0)
            k_e = jnp.concatenate([k[:half] * below, k[half:] / above], axis=0)
            k_t = k_e * dec_hi
            q_i = q_e * dec_mid
            s_t = st_ref[j]
            scores = lax.dot_general(q_e.astype(BF16), k_e.astype(BF16), _NT,
                                     preferred_element_type=F32)
            probs = jnp.where(causal, scores, 0.0).astype(BF16)
            o = jnp.dot(probs, v.astype(BF16), preferred_element_type=F32)
            o = o + lax.dot_general(q_i.astype(BF16), s_t.astype(BF16), _NT,
                                    preferred_element_type=F32)
            st_ref[j] = (dec_mid * dec_hi) * s_t + jnp.dot(
                v.T.astype(BF16), k_t.astype(BF16), preferred_element_type=F32)
            yb = _lane_rms(o, nb) * _silu(strip(7, rows))
            mix_write[rows, mix_lo + hd:mix_lo + 2 * hd] = yb.astype(BF16)

        tail = cv_ref[j]
        for i in range(PROMPT_COL_BLOCKS):
            if do_project:
                p_write[g, i, :, 0:PROMPT_COLS] = jnp.dot(
                    xb_ref[:, 0:x_ref.shape[-1]], w_ref[g, i], preferred_element_type=F32)
            for c in range(chunks_per_block):
                rows = pl.ds((i * chunks_per_block + c) * CHUNK, CHUNK)
                tail = conv_chunk(rows, tail)
                hgrn_chunk(rows)
        cv_ref[j] = tail
        return tail

    steady = (n >= 2) & (n < units)
    _run_variants([(n == 0, 0, (True, False, False)),
                   (n == 1, 1, (True, True, False)),
                   (steady & (n % 2 == 0), 0, (True, True, True)),
                   (steady & (n % 2 == 1), 1, (True, True, True)),
                   (n == units, units % 2, (False, True, True)),
                   (n == units + 1, (units + 1) % 2, (False, False, True))], stage)

    @pl.when((n >= 2) & ((n - 2) % per_tile == per_tile - 1))
    def _():
        _layer_norm_rows(y_ref, lg_ref[...], lbias_ref[...], tile)


def _sample_body(x_ref, hist_ref, s0_ref, wa_ref, wb_ref, woa_ref, wob_ref,
                 cw_ref, na_ref, lbl_ref, nb_ref, lg_ref, lbias_ref,
                 y_ref, cbuf_ref, snew_ref, wr_ref, wor_ref,
                 xb_ref, p_even, p_odd, mix_even, mix_odd, wo_b,
                 *, seqs, steps, pairs, alpha):
    jj = pl.program_id(0)
    c = pl.program_id(1)
    hd = LANES
    rows_all, d_model = x_ref.shape
    nrow = seqs * steps
    out_cols = d_model // COL_BLOCKS

    @pl.when((jj == 0) & (c == 0))
    def _():
        _cast_rows(x_ref, xb_ref, rows_all)
        _scale_rows(x_ref, y_ref, alpha, rows_all)

    @pl.when((jj >= 2) & (c == 0))
    def _():
        for ref, lo in ((woa_ref, 0), (wob_ref, hd)):
            wor_ref[lo:lo + hd, :] = ref[...].astype(BF16)
            for q in range(COL_BLOCKS):
                wo_b[q, lo:lo + hd, :] = ref[:, q * out_cols:(q + 1) * out_cols].astype(BF16)

    def stage(parity, do_project, do_elementwise, do_output):
        p_write, p_read = (p_even, p_odd) if parity == 0 else (p_odd, p_even)
        mix_write, mix_read = (mix_even, mix_odd) if parity == 0 else (mix_odd, mix_even)

        if do_project:
            wr_ref[:, 0:hd] = wa_ref[...].astype(BF16)
            wr_ref[:, hd:2 * hd] = wb_ref[...].astype(BF16)
            p_write[c] = jnp.dot(xb_ref[:, 0:x_ref.shape[-1]], wr_ref[...], preferred_element_type=F32)
        if do_output:
            cols = pl.ds(pl.multiple_of(c * out_cols, out_cols), out_cols)
            y_ref[:, cols] += jnp.dot(mix_read[...], wo_b[c], preferred_element_type=F32)
        if not do_elementwise:
            return

        rows = pl.ds(pl.multiple_of(c * nrow, nrow), nrow)
        strip = _strip_reader(p_read)
        pos = lax.broadcasted_iota(jnp.int32, (nrow, hd), 0) % steps
        per_row = lambda a: jnp.broadcast_to(a, (seqs, steps, hd)).reshape(nrow, hd)

        hist = hist_ref[...]
        prev2 = per_row(hist[:, 0:1, :])
        prev1 = per_row(hist[:, 1:2, :])
        u = strip(2, rows) * strip(0, rows)
        u1 = jnp.where(pos == 0, prev1, pltpu.roll(u, 1, axis=0))
        u2 = jnp.where(pos == 0, prev2, jnp.where(pos == 1, prev1, pltpu.roll(u, 2, axis=0)))
        ya = _conv_apply(u, u1, u2, cw_ref[...], strip(1, rows), strip(3, rows), na_ref[...])
        mix_write[rows, 0:hd] = ya.astype(BF16)
        cbuf_ref[...] = u.reshape(seqs, steps, hd)[:, steps - (CONV_WIDTH - 1):, :]

        lb = _lower_bound(lbl_ref[...])
        q, k, g = _gates(strip(4, rows), strip(5, rows), lb, hd)
        v = strip(6, rows)
        cum = _segment_cumsum(g, pos, steps)
        c_last = per_row(cum.reshape(seqs, steps, hd)[:, steps - 1:steps, :])
        q_e = q * jnp.exp2(cum)
        k_e = k * jnp.exp2(-cum)
        k_t = k * jnp.exp2(c_last - cum)
        decay = jnp.exp2(c_last)

        ri = lax.broadcasted_iota(jnp.int32, (nrow, nrow), 0)
        ci = lax.broadcasted_iota(jnp.int32, (nrow, nrow), 1)
        same_causal = (ri // steps == ci // steps) & (ri >= ci)
        scores = lax.dot_general(q_e.astype(BF16), k_e.astype(BF16), _NT,
                                 preferred_element_type=F32)
        probs = jnp.where(same_causal, scores, 0.0).astype(BF16)
        o_intra = jnp.dot(probs, v.astype(BF16), preferred_element_type=F32)

        o_inter = []
        for s in range(seqs):
            sr = slice(s * steps, (s + 1) * steps)
            s0 = s0_ref[s]
            o_inter.append(jnp.dot(q_e[sr].astype(BF16), s0.astype(BF16),
                                   preferred_element_type=F32))
            upd = lax.dot_general(k_t[sr].astype(BF16), v[sr].astype(BF16), _TN,
                                  preferred_element_type=F32)
            decay_col = jnp.broadcast_to(decay[s * steps:s * steps + 1], (hd, hd)).T
            snew_ref[s] = decay_col * s0 + upd
        o = o_intra + jnp.concatenate(o_inter, axis=0)

        yb = _lane_rms(o, nb_ref[...]) * _silu(strip(7, rows))
        mix_write[rows, hd:2 * hd] = yb.astype(BF16)

    steady = (jj >= 2) & (jj < pairs)
    _run_variants([(jj == 0, 0, (True, False, False)),
                   (jj == 1, 1, (True, True, False)),
                   (steady & (jj % 2 == 0), 0, (True, True, True)),
                   (steady & (jj % 2 == 1), 1, (True, True, True)),
                   (jj == pairs, pairs % 2, (False, True, True)),
                   (jj == pairs + 1, (pairs + 1) % 2, (False, False, True))], stage)

    @pl.when((jj == pairs + 1) & (c == pl.num_programs(1) - 1))
    def _():
        _layer_norm_rows(y_ref, lg_ref[...], lbias_ref[...], rows_all)


def kernel(x_prompt, x_sample, state_conv, state_hgrn, w_in, conv_w, norm_a, lb_logits,
           norm_b, w_out, ln_gain, ln_bias):
    batch, seq, d_model = x_prompt.shape
    dec_batch, dec_seq, _ = x_sample.shape
    depth, _, n_proj = w_in.shape
    assert depth == 1
    heads, dk, dv = state_hgrn.shape[2:]
    w_conv = state_conv.shape[-1]
    hd = LANES
    assert dk == hd and dv == hd and w_conv == heads * hd and n_proj == STRIPS * heads * hd
    assert dec_seq == SUBLANES and dec_batch // SAMPLE_SEQS == COL_BLOCKS
    assert seq % PROMPT_TILE == 0 and PROMPT_TILE % (PROMPT_COL_BLOCKS * CHUNK) == 0
    wide = PROMPT_COLS // MXU_COLS
    assert heads % PROMPT_GROUP == 0 and heads // PROMPT_GROUP > 2
    assert d_model % (COL_BLOCKS * LANES) == 0
    alpha = (2.0 * depth) ** 0.25
    f32 = x_prompt.dtype
    last = heads - 1

    def strip_spec(arr_rows, pair_of):
        return pl.BlockSpec((arr_rows, hd), lambda *g: (0, pair_of(*g)))

    rows_s = dec_batch * dec_seq
    chunks = dec_batch // SAMPLE_SEQS
    xs = x_sample.reshape(rows_s, d_model)
    mm_s = lambda jj, c: jnp.minimum(jj, last)
    ew_s = lambda jj, c: jnp.clip(jj - 1, 0, last)
    out_s = lambda jj, c: jnp.clip(jj - 2, 0, last)
    mm_c = lambda jj, c: jnp.where(jj > last, chunks - 1, c)
    ew_c = lambda jj, c: jnp.where(jj == 0, 0, jnp.where(jj > heads, chunks - 1, c))
    resident = lambda shape: pl.BlockSpec(shape, lambda jj, c: (0, 0),
                                          pipeline_mode=pl.Buffered(1))
    state_spec = pl.BlockSpec((None, SAMPLE_SEQS, None, hd, hd),
                              lambda jj, c: (0, ew_c(jj, c), ew_s(jj, c), 0, 0))
    hist_spec = pl.BlockSpec((None, SAMPLE_SEQS, CONV_WIDTH - 1, hd),
                             lambda jj, c: (0, ew_c(jj, c), 0, ew_s(jj, c)))
    w_strip = lambda half: pl.BlockSpec(
        (d_model, hd), lambda jj, c: (0, (2 * mm_c(jj, c) + half) * heads + mm_s(jj, c)))
    wo_strip = lambda half: pl.BlockSpec(
        (hd, d_model), lambda jj, c: (half * heads + out_s(jj, c), 0))
    y_s, conv_s, hgrn_s, w_r, wo_r = pl.pallas_call(
        functools.partial(_sample_body, seqs=SAMPLE_SEQS, steps=dec_seq, pairs=heads,
                          alpha=alpha),
        grid=(heads + 2, chunks),
        in_specs=[
            resident((rows_s, d_model)),
            hist_spec,
            state_spec,
            w_strip(0), w_strip(1), wo_strip(0), wo_strip(1),
            pl.BlockSpec((None, CONV_WIDTH, hd), lambda jj, c: (0, 0, ew_s(jj, c))),
            strip_spec(1, ew_s), strip_spec(lb_logits.shape[0], ew_s), strip_spec(1, ew_s),
            pl.BlockSpec((1, d_model), lambda jj, c: (0, 0)),
            pl.BlockSpec((1, d_model), lambda jj, c: (0, 0)),
        ],
        out_specs=[
            resident((rows_s, d_model)),
            hist_spec,
            state_spec,
            pl.BlockSpec((None, None, d_model, MXU_COLS),
                         lambda jj, c: (mm_s(jj, c), mm_c(jj, c) // wide, 0, mm_c(jj, c) % wide)),
            pl.BlockSpec((None, 2 * hd, d_model), lambda jj, c: (out_s(jj, c), 0, 0)),
        ],
        out_shape=[
            jax.ShapeDtypeStruct((rows_s, d_model), f32),
            jax.ShapeDtypeStruct((depth, dec_batch, CONV_WIDTH - 1, w_conv), state_conv.dtype),
            jax.ShapeDtypeStruct((depth, dec_batch, heads, dk, dv), state_hgrn.dtype),
            jax.ShapeDtypeStruct((heads, PROMPT_COL_BLOCKS, d_model, PROMPT_COLS), BF16),
            jax.ShapeDtypeStruct((heads, 2 * hd, d_model), BF16),
        ],
        scratch_shapes=[
            pltpu.VMEM((rows_s, d_model + LANES), BF16),
            pltpu.VMEM((COL_BLOCKS, rows_s, MXU_COLS), F32),
            pltpu.VMEM((COL_BLOCKS, rows_s, MXU_COLS), F32),
            pltpu.VMEM((rows_s, 2 * hd), BF16),
            pltpu.VMEM((rows_s, 2 * hd), BF16),
            pltpu.VMEM((COL_BLOCKS, 2 * hd, d_model // COL_BLOCKS), BF16),
        ],
        compiler_params=pltpu.CompilerParams(
            dimension_semantics=("arbitrary", "arbitrary"),
            vmem_limit_bytes=VMEM_LIMIT),
        name="sample_mixer",
    )(xs, state_conv, state_hgrn, w_in[0], w_in[0], w_out[0], w_out[0],
      conv_w, norm_a, lb_logits, norm_b, ln_gain, ln_bias)

    tile = PROMPT_TILE
    tiles_per_seq = seq // tile
    group = PROMPT_GROUP
    per_tile = heads // group
    units = batch * tiles_per_seq * per_tile
    unit_p = lambda n: jnp.minimum(n, units - 1)
    unit_e = lambda n: jnp.clip(n - 1, 0, units - 1)
    unit_o = lambda n: jnp.clip(n - 2, 0, units - 1)
    group_p = lambda n: unit_p(n) % per_tile
    group_e = lambda n: unit_e(n) % per_tile
    group_o = lambda n: unit_o(n) % per_tile
    batch_e = lambda n: unit_e(n) // (per_tile * tiles_per_seq)

    def row_tile_map(unit_of):
        def index_map(n):
            row_tile = unit_of(n) // per_tile
            return (row_tile // tiles_per_seq, row_tile % tiles_per_seq, 0)
        return index_map

    def group_strip(arr_rows):
        return pl.BlockSpec((arr_rows, group * hd), lambda n: (0, group_e(n)))

    y_p, conv_p, hgrn_p = pl.pallas_call(
        functools.partial(_prompt_body, tile=tile, tiles_per_seq=tiles_per_seq, pairs=heads,
                          group=group, units=units, alpha=alpha),
        grid=(units + 2,),
        in_specs=[
            pl.BlockSpec((None, tile, d_model), row_tile_map(unit_p)),
            pl.BlockSpec((group, PROMPT_COL_BLOCKS, d_model, PROMPT_COLS),
                         lambda n: (group_p(n), 0, 0, 0)),
            pl.BlockSpec((group * 2 * hd, d_model), lambda n: (group_o(n), 0)),
            pl.BlockSpec((None, CONV_WIDTH, group * hd), lambda n: (0, 0, group_e(n))),
            group_strip(1), group_strip(lb_logits.shape[0]), group_strip(1),
            pl.BlockSpec((1, d_model), lambda n: (0, 0)),
            pl.BlockSpec((1, d_model), lambda n: (0, 0)),
        ],
        out_specs=[
            pl.BlockSpec((None, tile, d_model), row_tile_map(unit_o)),
            pl.BlockSpec((None, None, CONV_WIDTH - 1, w_conv), lambda n: (0, batch_e(n), 0, 0)),
            pl.BlockSpec((None, None, heads, hd, hd), lambda n: (0, batch_e(n), 0, 0, 0)),
        ],
        out_shape=[
            jax.ShapeDtypeStruct((batch, seq, d_model), f32),
            jax.ShapeDtypeStruct((depth, batch, CONV_WIDTH - 1, w_conv), state_conv.dtype),
            jax.ShapeDtypeStruct((depth, batch, heads, dk, dv), state_hgrn.dtype),
        ],
        scratch_shapes=[
            pltpu.VMEM((tile, d_model + LANES), BF16),
            pltpu.VMEM((group, PROMPT_COL_BLOCKS, tile, PROMPT_COLS + LANES), F32),
            pltpu.VMEM((group, PROMPT_COL_BLOCKS, tile, PROMPT_COLS + LANES), F32),
            pltpu.VMEM((tile, group * 2 * hd), BF16),
            pltpu.VMEM((tile, group * 2 * hd), BF16),
            pltpu.VMEM((heads, hd, hd), F32),
            pltpu.VMEM((heads, SUBLANES, hd), F32),
        ],
        compiler_params=pltpu.CompilerParams(
            dimension_semantics=("arbitrary",),
            vmem_limit_bytes=VMEM_LIMIT),
        name="prompt_mixer",
    )(x_prompt, w_r, wo_r.reshape(heads * 2 * hd, d_model), conv_w, norm_a, lb_logits, norm_b,
      ln_gain, ln_bias)

    return (y_p, y_s.reshape(x_sample.shape), conv_p, hgrn_p, conv_s, hgrn_s)
```

```python
import functools

import jax
import jax.numpy as jnp
from jax import lax
from jax.experimental import pallas as pl
from jax.experimental.pallas import tpu as pltpu

LANES = 128
SUBLANES = 8
MXU_COLS = 256
STRIPS = 8
COL_BLOCKS = STRIPS * LANES // MXU_COLS
PROMPT_COLS = 2 * MXU_COLS
PROMPT_COL_BLOCKS = STRIPS * LANES // PROMPT_COLS
CONV_WIDTH = 3
EPS = 1e-5
PROMPT_TILE = 512
PROMPT_GROUP = 2
CHUNK = 128
SAMPLE_SEQS = 32
VMEM_LIMIT = 56 * 1024 * 1024

F32 = jnp.float32
BF16 = jnp.bfloat16
_NT = (((1,), (1,)), ((), ()))
_TN = (((0,), (0,)), ((), ()))


def _silu(x, scale=1.0):
    half = 0.5 * x
    a = half if scale == 1.0 else (0.5 * scale) * x
    return a + a * jnp.tanh(half)


def _lane_rms(x, gain):
    return x * lax.rsqrt(jnp.mean(x * x, axis=-1, keepdims=True) + EPS) * gain


def _lower_bound(lbl):
    e = jnp.exp(lbl - jnp.max(lbl, axis=0, keepdims=True))
    return e[0:1] / jnp.sum(e, axis=0, keepdims=True)


def _running_product(p, pos, reverse=False):
    n = p.shape[0]
    s = 1
    while s < n:
        if s < SUBLANES:
            if reverse:
                p = p * jnp.where(pos < n - s, pltpu.roll(p, n - s, axis=0), 1.0)
            else:
                p = p * jnp.where(pos >= s, pltpu.roll(p, s, axis=0), 1.0)
        elif reverse:
            p = jnp.concatenate([p[:n - s] * p[s:], p[n - s:]], axis=0)
        else:
            p = jnp.concatenate([p[:s], p[s:] * p[:n - s]], axis=0)
        s *= 2
    return p


def _layer_norm_rows(y_ref, gain, bias, rows):
    def body(i, _):
        r0 = pl.multiple_of(i * CHUNK, CHUNK)
        z = y_ref[pl.ds(r0, CHUNK), :]
        mu = jnp.mean(z, axis=-1, keepdims=True)
        zc = z - mu
        var = jnp.mean(zc * zc, axis=-1, keepdims=True)
        y_ref[pl.ds(r0, CHUNK), :] = zc * lax.rsqrt(var + EPS) * gain + bias
        return 0
    lax.fori_loop(0, rows // CHUNK, body, 0)


def _cast_rows(x_ref, xb_ref, rows):
    def body(i, _):
        r0 = pl.multiple_of(i * CHUNK, CHUNK)
        xb_ref[pl.ds(r0, CHUNK), 0:x_ref.shape[-1]] = x_ref[pl.ds(r0, CHUNK), :].astype(BF16)
        return 0
    lax.fori_loop(0, rows // CHUNK, body, 0)


def _scale_rows(x_ref, y_ref, alpha, rows):
    def body(i, _):
        r0 = pl.multiple_of(i * CHUNK, CHUNK)
        y_ref[pl.ds(r0, CHUNK), :] = alpha * x_ref[pl.ds(r0, CHUNK), :]
        return 0
    lax.fori_loop(0, rows // CHUNK, body, 0)


def _forget_gate(fb, lb):
    return (0.5 + 0.5 * lb) + (0.5 - 0.5 * lb) * jnp.tanh(0.5 * fb)


def _strip_reader(p_ref):
    def strip(k, rows):
        lo = (k % 2) * LANES
        return p_ref[k // 2, rows, lo:lo + LANES]
    return strip


def _conv_apply(u, u1, u2, cw, b_a, z_a, na):
    conv = cw[0:1] * u2 + cw[1:2] * u1 + cw[2:3] * u
    return _lane_rms(b_a * conv, na) * _silu(z_a)


def _run_variants(variants, stage):
    for cond, parity, flags in variants:
        pl.when(cond)(functools.partial(stage, parity, *flags))


def _prompt_body(x_ref, w_ref, wo_ref, cw_ref, na_ref, lbl_ref, nb_ref, lg_ref, lbias_ref,
                 y_ref, cbuf_ref, snew_ref,
                 xb_ref, p_even, p_odd, mix_even, mix_odd, st_ref, cv_ref,
                 *, tile, tiles_per_seq, pairs, group, units, alpha):
    n = pl.program_id(0)
    hd = LANES
    chunks_per_block = tile // (PROMPT_COL_BLOCKS * CHUNK)
    strips_per_block = PROMPT_COLS // hd
    per_tile = pairs // group
    unit_e = jnp.clip(n - 1, 0, units - 1)
    j0 = (unit_e % per_tile) * group
    row_tile = unit_e // per_tile
    t = row_tile % tiles_per_seq

    @pl.when((n % per_tile == 0) & (n < units))
    def _():
        _cast_rows(x_ref, xb_ref, tile)

    @pl.when(n % per_tile == 2)
    def _():
        _scale_rows(x_ref, y_ref, alpha, tile)

    @pl.when((n >= 1) & (j0 == 0) & (t == 0))
    def _():
        st_ref[...] = jnp.zeros_like(st_ref)
        cv_ref[...] = jnp.zeros_like(cv_ref)

    def stage(parity, do_project, do_elementwise, do_output):
        p_write, p_read = (p_even, p_odd) if parity == 0 else (p_odd, p_even)
        mix_write, mix_read = (mix_even, mix_odd) if parity == 0 else (mix_odd, mix_even)

        def project(g, i):
            p_write[g, i, :, 0:PROMPT_COLS] = jnp.dot(
                    xb_ref[:, 0:x_ref.shape[-1]], w_ref[g, i], preferred_element_type=F32)

        def output():
            y_ref[...] += jnp.dot(mix_read[...], wo_ref[...], preferred_element_type=F32)

        if not do_elementwise:
            if do_project:
                for g in range(group):
                    for i in range(PROMPT_COL_BLOCKS):
                        project(g, i)
            if do_output:
                output()
            return
        pos = lax.broadcasted_iota(jnp.int32, (CHUNK, hd), 0)
        causal = (lax.broadcasted_iota(jnp.int32, (CHUNK, CHUNK), 0)
                  >= lax.broadcasted_iota(jnp.int32, (CHUNK, CHUNK), 1))
        mid = CHUNK // 2 - 1
        tails = [elementwise_unit(g, p_write, p_read, mix_write, do_project, pos, causal, mid)
                 for g in range(group)]
        if do_output:
            output()

        @pl.when(t == tiles_per_seq - 1)
        def _():
            for g in range(group):
                lanes = pl.ds(pl.multiple_of((j0 + g) * hd, hd), hd)
                cbuf_ref[:, lanes] = tails[g][SUBLANES - (CONV_WIDTH - 1):SUBLANES]
                snew_ref[j0 + g] = st_ref[j0 + g].T

    def elementwise_unit(g, p_write, p_read, mix_write, do_project, pos, causal, mid):
        j = j0 + g
        lanes = slice(g * hd, (g + 1) * hd)
        mix_lo = g * 2 * hd

        def strip(k, rows):
            lo = (k % strips_per_block) * hd
            return p_read[g, k // strips_per_block, rows, lo:lo + hd]

        cw = cw_ref[:, lanes]
        na = na_ref[:, lanes]
        nb = nb_ref[:, lanes]
        lb = _lower_bound(lbl_ref[:, lanes])

        def conv_chunk(rows, tail):
            u = strip(2, rows) * strip(0, rows)
            head = (SUBLANES, hd)
            prev1 = jnp.broadcast_to(tail[SUBLANES - 1:SUBLANES], head)
            prev2 = jnp.broadcast_to(tail[SUBLANES - 2:SUBLANES - 1], head)
            pos8 = pos[:SUBLANES]
            u1 = pltpu.roll(u, 1, axis=0)
            u1 = jnp.concatenate([jnp.where(pos8 == 0, prev1, u1[:SUBLANES]), u1[SUBLANES:]], axis=0)
            u2 = pltpu.roll(u, 2, axis=0)
            u2_head = jnp.where(pos8 == 0, prev2, jnp.where(pos8 == 1, prev1, u2[:SUBLANES]))
            u2 = jnp.concatenate([u2_head, u2[SUBLANES:]], axis=0)
            ya = _conv_apply(u, u1, u2, cw, strip(1, rows), strip(3, rows), na)
            mix_write[rows, mix_lo:mix_lo + hd] = ya.astype(BF16)
            return u[CHUNK - SUBLANES:CHUNK]

        def hgrn_chunk(rows):
            q = _silu(strip(4, rows), hd ** -0.5)
            f = _forget_gate(strip(5, rows), lb)
            k = 1.0 - f
            v = strip(6, rows)
            half = CHUNK // 2
            pos_h = pos[:half]
            f_lo, f_hi = f[:half], f[half:]
            nxt = jnp.where(pos_h == half - 1, 1.0, pltpu.roll(f_lo, half - 1, axis=0))
            below = _running_product(nxt, pos_h, reverse=True)
            above = _running_product(f_hi, pos_h)
            dec_mid = f_lo[0:1] * below[0:1]
            dec_hi = above[half - 1:half]
            q_e = jnp.concatenate([q[:half] / below, q[half:] * above], axis=0)
            k_e = jnp.concatenate([k[:half] * below, k[half:] / above], axis=0)
            k_t = k_e * dec_hi
            q_i = q_e * dec_mid
            s_t = st_ref[j]
            scores = lax.dot_general(q_e.astype(BF16), k_e.astype(BF16), _NT,
                                     preferred_element_type=F32)
            probs = jnp.where(causal, scores, 0.0).astype(BF16)
            o = jnp.dot(probs, v.astype(BF16), preferred_element_type=F32)
            o = o + lax.dot_general(q_i.astype(BF16), s_t.astype(BF16), _NT,
                                    preferred_element_type=F32)
            st_ref[j] = (dec_mid * dec_hi) * s_t + jnp.dot(
                v.T.astype(BF16), k_t.astype(BF16), preferred_element_type=F32)
            yb = _lane_rms(o, nb) * _silu(strip(7, rows))
            mix_write[rows, mix_lo + hd:mix_lo + 2 * hd] = yb.astype(BF16)

        tail = cv_ref[j]
        for i in range(PROMPT_COL_BLOCKS):
            if do_project:
                p_write[g, i, :, 0:PROMPT_COLS] = jnp.dot(
                    xb_ref[:, 0:x_ref.shape[-1]], w_ref[g, i], preferred_element_type=F32)
            for c in range(chunks_per_block):
                rows = pl.ds((i * chunks_per_block + c) * CHUNK, CHUNK)
                tail = conv_chunk(rows, tail)
                hgrn_chunk(rows)
        cv_ref[j] = tail
        return tail

    steady = (n >= 2) & (n < units)
    _run_variants([(n == 0, 0, (True, False, False)),
                   (n == 1, 1, (True, True, False)),
                   (steady & (n % 2 == 0), 0, (True, True, True)),
                   (steady & (n % 2 == 1), 1, (True, True, True)),
                   (n == units, units % 2, (False, True, True)),
                   (n == units + 1, (units + 1) % 2, (False, False, True))], stage)

    @pl.when((n >= 2) & ((n - 2) % per_tile == per_tile - 1))
    def _():
        _layer_norm_rows(y_ref, lg_ref[...], lbias_ref[...], tile)


def _sample_body(x_ref, hist_ref, s0_ref, wa_ref, wb_ref, woa_ref, wob_ref,
                 cw_ref, na_ref, lbl_ref, nb_ref, lg_ref, lbias_ref,
                 y_ref, cbuf_ref, snew_ref, wr_ref, wor_ref,
                 xb_ref, p_even, p_odd, mix_even, mix_odd, wo_b,
                 *, seqs, steps, pairs, alpha):
    jj = pl.program_id(0)
    c = pl.program_id(1)
    hd = LANES
    rows_all, d_model = x_ref.shape
    nrow = seqs * steps
    out_cols = d_model // COL_BLOCKS

    @pl.when((jj == 0) & (c == 0))
    def _():
        _cast_rows(x_ref, xb_ref, rows_all)
        _scale_rows(x_ref, y_ref, alpha, rows_all)

    @pl.when((jj >= 2) & (c == 0))
    def _():
        for ref, lo in ((woa_ref, 0), (wob_ref, hd)):
            wor_ref[lo:lo + hd, :] = ref[...].astype(BF16)
            for q in range(COL_BLOCKS):
                wo_b[q, lo:lo + hd, :] = ref[:, q * out_cols:(q + 1) * out_cols].astype(BF16)

    def stage(parity, do_project, do_elementwise, do_output):
        p_write, p_read = (p_even, p_odd) if parity == 0 else (p_odd, p_even)
        mix_write, mix_read = (mix_even, mix_odd) if parity == 0 else (mix_odd, mix_even)

        if do_project:
            wr_ref[:, 0:hd] = wa_ref[...].astype(BF16)
            wr_ref[:, hd:2 * hd] = wb_ref[...].astype(BF16)
            p_write[c] = jnp.dot(xb_ref[:, 0:x_ref.shape[-1]], wr_ref[...], preferred_element_type=F32)
        if do_output:
            cols = pl.ds(pl.multiple_of(c * out_cols, out_cols), out_cols)
            y_ref[:, cols] += jnp.dot(mix_read[...], wo_b[c], preferred_element_type=F32)
        if not do_elementwise:
            return

        rows = pl.ds(pl.multiple_of(c * nrow, nrow), nrow)
        strip = _strip_reader(p_read)
        pos = lax.broadcasted_iota(jnp.int32, (nrow, hd), 0) % steps
        per_row = lambda a: jnp.broadcast_to(a, (seqs, steps, hd)).reshape(nrow, hd)

        hist = hist_ref[...]
        prev2 = per_row(hist[:, 0:1, :])
        prev1 = per_row(hist[:, 1:2, :])
        u = strip(2, rows) * strip(0, rows)
        u1 = jnp.where(pos == 0, prev1, pltpu.roll(u, 1, axis=0))
        u2 = jnp.where(pos == 0, prev2, jnp.where(pos == 1, prev1, pltpu.roll(u, 2, axis=0)))
        ya = _conv_apply(u, u1, u2, cw_ref[...], strip(1, rows), strip(3, rows), na_ref[...])
        mix_write[rows, 0:hd] = ya.astype(BF16)
        cbuf_ref[...] = u.reshape(seqs, steps, hd)[:, steps - (CONV_WIDTH - 1):, :]

        lb = _lower_bound(lbl_ref[...])
        q = _silu(strip(4, rows), hd ** -0.5)
        f = _forget_gate(strip(5, rows), lb)
        k = 1.0 - f
        v = strip(6, rows)
        run = f
        for s in (1, 2, 4):
            if s < steps:
                run = run * jnp.where(pos >= s, pltpu.roll(run, s, axis=0), 1.0)
        decay = per_row(run.reshape(seqs, steps, hd)[:, steps - 1:steps, :])
        q_e = q * run
        k_e = k / run
        k_t = k_e * decay

        ri = lax.broadcasted_iota(jnp.int32, (nrow, nrow), 0)
        ci = lax.broadcasted_iota(jnp.int32, (nrow, nrow), 1)
        same_causal = (ri // steps == ci // steps) & (ri >= ci)
        scores = lax.dot_general(q_e.astype(BF16), k_e.astype(BF16), _NT,
                                 preferred_element_type=F32)
        probs = jnp.where(same_causal, scores, 0.0).astype(BF16)
        o_intra = jnp.dot(probs, v.astype(BF16), preferred_element_type=F32)

        o_inter = []
        for s in range(seqs):
            sr = slice(s * steps, (s + 1) * steps)
            s0 = s0_ref[s]
            o_inter.append(jnp.dot(q_e[sr].astype(BF16), s0.astype(BF16),
                                   preferred_element_type=F32))
            upd = lax.dot_general(k_t[sr].astype(BF16), v[sr].astype(BF16), _TN,
                                  preferred_element_type=F32)
            decay_col = jnp.broadcast_to(decay[s * steps:s * steps + 1], (hd, hd)).T
            snew_ref[s] = decay_col * s0 + upd
        o = o_intra + jnp.concatenate(o_inter, axis=0)

        yb = _lane_rms(o, nb_ref[...]) * _silu(strip(7, rows))
        mix_write[rows, hd:2 * hd] = yb.astype(BF16)

    steady = (jj >= 2) & (jj < pairs)
    _run_variants([(jj == 0, 0, (True, False, False)),
                   (jj == 1, 1, (True, True, False)),
                   (steady & (jj % 2 == 0), 0, (True, True, True)),
                   (steady & (jj % 2 == 1), 1, (True, True, True)),
                   (jj == pairs, pairs % 2, (False, True, True)),
                   (jj == pairs + 1, (pairs + 1) % 2, (False, False, True))], stage)

    @pl.when((jj == pairs + 1) & (c == pl.num_programs(1) - 1))
    def _():
        _layer_norm_rows(y_ref, lg_ref[...], lbias_ref[...], rows_all)


def kernel(x_prompt, x_sample, state_conv, state_hgrn, w_in, conv_w, norm_a, lb_logits,
           norm_b, w_out, ln_gain, ln_bias):
    batch, seq, d_model = x_prompt.shape
    dec_batch, dec_seq, _ = x_sample.shape
    depth, _, n_proj = w_in.shape
    assert depth == 1
    heads, dk, dv = state_hgrn.shape[2:]
    w_conv = state_conv.shape[-1]
    hd = LANES
    assert dk == hd and dv == hd and w_conv == heads * hd and n_proj == STRIPS * heads * hd
    assert dec_seq == SUBLANES and dec_batch // SAMPLE_SEQS == COL_BLOCKS
    assert seq % PROMPT_TILE == 0 and PROMPT_TILE % (PROMPT_COL_BLOCKS * CHUNK) == 0
    wide = PROMPT_COLS // MXU_COLS
    assert heads % PROMPT_GROUP == 0 and heads // PROMPT_GROUP > 2
    assert d_model % (COL_BLOCKS * LANES) == 0
    alpha = (2.0 * depth) ** 0.25
    f32 = x_prompt.dtype
    last = heads - 1

    def strip_spec(arr_rows, pair_of):
        return pl.BlockSpec((arr_rows, hd), lambda *g: (0, pair_of(*g)))

    rows_s = dec_batch * dec_seq
    chunks = dec_batch // SAMPLE_SEQS
    xs = x_sample.reshape(rows_s, d_model)
    mm_s = lambda jj, c: jnp.minimum(jj, last)
    ew_s = lambda jj, c: jnp.clip(jj - 1, 0, last)
    out_s = lambda jj, c: jnp.clip(jj - 2, 0, last)
    mm_c = lambda jj, c: jnp.where(jj > last, chunks - 1, c)
    ew_c = lambda jj, c: jnp.where(jj == 0, 0, jnp.where(jj > heads, chunks - 1, c))
    resident = lambda shape: pl.BlockSpec(shape, lambda jj, c: (0, 0),
                                          pipeline_mode=pl.Buffered(1))
    state_spec = pl.BlockSpec((None, SAMPLE_SEQS, None, hd, hd),
                              lambda jj, c: (0, ew_c(jj, c), ew_s(jj, c), 0, 0))
    hist_spec = pl.BlockSpec((None, SAMPLE_SEQS, CONV_WIDTH - 1, hd),
                             lambda jj, c: (0, ew_c(jj, c), 0, ew_s(jj, c)))
    w_strip = lambda half: pl.BlockSpec(
        (d_model, hd), lambda jj, c: (0, (2 * mm_c(jj, c) + half) * heads + mm_s(jj, c)))
    wo_strip = lambda half: pl.BlockSpec(
        (hd, d_model), lambda jj, c: (half * heads + out_s(jj, c), 0))
    y_s, conv_s, hgrn_s, w_r, wo_r = pl.pallas_call(
        functools.partial(_sample_body, seqs=SAMPLE_SEQS, steps=dec_seq, pairs=heads,
                          alpha=alpha),
        grid=(heads + 2, chunks),
        in_specs=[
            resident((rows_s, d_model)),
            hist_spec,
            state_spec,
            w_strip(0), w_strip(1), wo_strip(0), wo_strip(1),
            pl.BlockSpec((None, CONV_WIDTH, hd), lambda jj, c: (0, 0, ew_s(jj, c))),
            strip_spec(1, ew_s), strip_spec(lb_logits.shape[0], ew_s), strip_spec(1, ew_s),
            pl.BlockSpec((1, d_model), lambda jj, c: (0, 0)),
            pl.BlockSpec((1, d_model), lambda jj, c: (0, 0)),
        ],
        out_specs=[
            resident((rows_s, d_model)),
            hist_spec,
            state_spec,
            pl.BlockSpec((None, None, d_model, MXU_COLS),
                         lambda jj, c: (mm_s(jj, c), mm_c(jj, c) // wide, 0, mm_c(jj, c) % wide)),
            pl.BlockSpec((None, 2 * hd, d_model), lambda jj, c: (out_s(jj, c), 0, 0)),
        ],
        out_shape=[
            jax.ShapeDtypeStruct((rows_s, d_model), f32),
            jax.ShapeDtypeStruct((depth, dec_batch, CONV_WIDTH - 1, w_conv), state_conv.dtype),
            jax.ShapeDtypeStruct((depth, dec_batch, heads, dk, dv), state_hgrn.dtype),
            jax.ShapeDtypeStruct((heads, PROMPT_COL_BLOCKS, d_model, PROMPT_COLS), BF16),
            jax.ShapeDtypeStruct((heads, 2 * hd, d_model), BF16),
        ],
        scratch_shapes=[
            pltpu.VMEM((rows_s, d_model + LANES), BF16),
            pltpu.VMEM((COL_BLOCKS, rows_s, MXU_COLS), F32),
            pltpu.VMEM((COL_BLOCKS, rows_s, MXU_COLS), F32),
            pltpu.VMEM((rows_s, 2 * hd), BF16),
            pltpu.VMEM((rows_s, 2 * hd), BF16),
            pltpu.VMEM((COL_BLOCKS, 2 * hd, d_model // COL_BLOCKS), BF16),
        ],
        compiler_params=pltpu.CompilerParams(
            dimension_semantics=("arbitrary", "arbitrary"),
            vmem_limit_bytes=VMEM_LIMIT),
        name="sample_mixer",
    )(xs, state_conv, state_hgrn, w_in[0], w_in[0], w_out[0], w_out[0],
      conv_w, norm_a, lb_logits, norm_b, ln_gain, ln_bias)

    tile = PROMPT_TILE
    tiles_per_seq = seq // tile
    group = PROMPT_GROUP
    per_tile = heads // group
    units = batch * tiles_per_seq * per_tile
    unit_p = lambda n: jnp.minimum(n, units - 1)
    unit_e = lambda n: jnp.clip(n - 1, 0, units - 1)
    unit_o = lambda n: jnp.clip(n - 2, 0, units - 1)
    group_p = lambda n: unit_p(n) % per_tile
    group_e = lambda n: unit_e(n) % per_tile
    group_o = lambda n: unit_o(n) % per_tile
    batch_e = lambda n: unit_e(n) // (per_tile * tiles_per_seq)

    def row_tile_map(unit_of):
        def index_map(n):
            row_tile = unit_of(n) // per_tile
            return (row_tile // tiles_per_seq, row_tile % tiles_per_seq, 0)
        return index_map

    def group_strip(arr_rows):
        return pl.BlockSpec((arr_rows, group * hd), lambda n: (0, group_e(n)))

    y_p, conv_p, hgrn_p = pl.pallas_call(
        functools.partial(_prompt_body, tile=tile, tiles_per_seq=tiles_per_seq, pairs=heads,
                          group=group, units=units, alpha=alpha),
        grid=(units + 2,),
        in_specs=[
            pl.BlockSpec((None, tile, d_model), row_tile_map(unit_p)),
            pl.BlockSpec((group, PROMPT_COL_BLOCKS, d_model, PROMPT_COLS),
                         lambda n: (group_p(n), 0, 0, 0)),
            pl.BlockSpec((group * 2 * hd, d_model), lambda n: (group_o(n), 0)),
            pl.BlockSpec((None, CONV_WIDTH, group * hd), lambda n: (0, 0, group_e(n))),
            group_strip(1), group_strip(lb_logits.shape[0]), group_strip(1),
            pl.BlockSpec((1, d_model), lambda n: (0, 0)),
            pl.BlockSpec((1, d_model), lambda n: (0, 0)),
        ],
        out_specs=[
            pl.BlockSpec((None, tile, d_model), row_tile_map(unit_o)),
            pl.BlockSpec((None, None, CONV_WIDTH - 1, w_conv), lambda n: (0, batch_e(n), 0, 0)),
            pl.BlockSpec((None, None, heads, hd, hd), lambda n: (0, batch_e(n), 0, 0, 0)),
        ],
        out_shape=[
            jax.ShapeDtypeStruct((batch, seq, d_model), f32),
            jax.ShapeDtypeStruct((depth, batch, CONV_WIDTH - 1, w_conv), state_conv.dtype),
            jax.ShapeDtypeStruct((depth, batch, heads, dk, dv), state_hgrn.dtype),
        ],
        scratch_shapes=[
            pltpu.VMEM((tile, d_model + LANES), BF16),
            pltpu.VMEM((group, PROMPT_COL_BLOCKS, tile, PROMPT_COLS + LANES), F32),
            pltpu.VMEM((group, PROMPT_COL_BLOCKS, tile, PROMPT_COLS + LANES), F32),
            pltpu.VMEM((tile, group * 2 * hd), BF16),
            pltpu.VMEM((tile, group * 2 * hd), BF16),
            pltpu.VMEM((heads, hd, hd), F32),
            pltpu.VMEM((heads, SUBLANES, hd), F32),
        ],
        compiler_params=pltpu.CompilerParams(
            dimension_semantics=("arbitrary",),
            vmem_limit_bytes=VMEM_LIMIT),
        name="prompt_mixer",
    )(x_prompt, w_r, wo_r.reshape(heads * 2 * hd, d_model), conv_w, norm_a, lb_logits, norm_b,
      ln_gain, ln_bias)

    return (y_p, y_s.reshape(x_sample.shape), conv_p, hgrn_p, conv_s, hgrn_s)
```
